```python
import jax, jax.numpy as jnp
from jax import lax
import numpy as np

D_MODEL = 2048
BATCH = 2
SEQ = 4096
DEPTH = 4
DEC_BATCH = 4
DEC_SEQ = 2048
PAST_LEN = 128

N_HEADS = 8
HEAD_DIM = 128
V_DIM = 2 * HEAD_DIM
QK_WIDTH = N_HEADS * 2 * HEAD_DIM
V_WIDTH = N_HEADS * V_DIM
CONV_CH = 2048
CONV_WIDTH = 31
CONV_PAD = (CONV_WIDTH - 1) // 2
N_BRANCH = 2
D_FF = -(-8 * D_MODEL // (3 * 256)) * 256
IN_WIDTH = 2 * QK_WIDTH + V_WIDTH + 2 * CONV_CH + N_BRANCH * D_MODEL
Q_BLOCK = 128
NORM_EPS = 1e-6
LN_EPS = 1e-5

kernel_name = "hybrid_diffattn_conformer_encoder"


def _rmsnorm(x, g):
    xf = x.astype(jnp.float32)
    y = xf * lax.rsqrt(jnp.mean(xf * xf, axis=-1, keepdims=True) + NORM_EPS)
    return (y * g.astype(jnp.float32)).astype(x.dtype)


def _layernorm(x, g, b):
    xf = x.astype(jnp.float32)
    xc = xf - jnp.mean(xf, axis=-1, keepdims=True)
    y = xc * lax.rsqrt(jnp.mean(xc * xc, axis=-1, keepdims=True) + LN_EPS)
    return (y * g.astype(jnp.float32) + b.astype(jnp.float32)).astype(x.dtype)


def _alibi_slopes():
    return jnp.asarray(2.0 ** (-8.0 * np.arange(1, N_HEADS + 1) / N_HEADS), dtype=jnp.float32)


def _lambda_init(layer):
    return 0.8 - 0.6 * float(np.exp(-0.3 * layer))


def _diff_attention(q, k, v, lam, g_subln, lam_init):
    b, s = q.shape[0], q.shape[1]
    n_blk = s // Q_BLOCK
    slopes = _alibi_slopes()
    scale = HEAD_DIM ** -0.5
    kpos = jnp.arange(s)
    q_blocks = q.reshape(b, n_blk, Q_BLOCK, N_HEADS, 2, HEAD_DIM).swapaxes(0, 1)

    def one_block(args):
        qb, i = args
        qpos = i * Q_BLOCK + jnp.arange(Q_BLOCK)
        dist = jnp.abs(qpos[:, None] - kpos[None, :]).astype(jnp.float32)
        bias = -slopes[:, None, None] * dist[None]
        sc = jnp.einsum("bqhmd,bkhmd->bhmqk", qb, k, preferred_element_type=jnp.float32)
        sc = sc * scale + bias[None, :, None]
        p = jax.nn.softmax(sc, axis=-1)
        w = p[:, :, 0] - lam * p[:, :, 1]
        return jnp.einsum("bhqk,bkhv->bqhv", w.astype(v.dtype), v)

    o = lax.map(one_block, (q_blocks, jnp.arange(n_blk)))
    o = o.swapaxes(0, 1).reshape(b, s, N_HEADS, V_DIM)
    o = _rmsnorm(o, g_subln) * (1.0 - lam_init)
    return o.reshape(b, s, V_WIDTH)


def _conformer_conv(u, w_dw, b_dw, g_ln, b_ln, w_pw, b_pw):
    val, gate = jnp.split(u, 2, axis=-1)
    y = val * jax.nn.sigmoid(gate)
    y = lax.conv_general_dilated(
        y, w_dw, window_strides=(1,), padding=[(CONV_PAD, CONV_PAD)],
        dimension_numbers=("NWC", "WIO", "NWC"), feature_group_count=CONV_CH) + b_dw
    y = jax.nn.silu(_layernorm(y, g_ln, b_ln))
    return y @ w_pw + b_pw


def _trunk(x, g_mix, w_in, b_gate, lambda_q, lambda_k, g_subln, w_attn_proj,
           w_dw, b_dw, g_conv_ln, b_conv_ln, w_conv_proj, b_conv_proj, w_out,
           g_ffn, w_ffn_in, w_ffn_out, g_final):
    b, s, _ = x.shape
    splits = np.cumsum([QK_WIDTH, QK_WIDTH, V_WIDTH, 2 * CONV_CH]).tolist()
    for l in range(DEPTH):
        h = _rmsnorm(x, g_mix[l])
        z = h @ w_in[l]
        zq, zk, zv, zu, zg = jnp.split(z, splits, axis=-1)
        q = zq.reshape(b, s, N_HEADS, 2, HEAD_DIM)
        k = zk.reshape(b, s, N_HEADS, 2, HEAD_DIM)
        v = zv.reshape(b, s, N_HEADS, V_DIM)
        lam_init = _lambda_init(l)
        lq = lambda_q[l].astype(jnp.float32)
        lk = lambda_k[l].astype(jnp.float32)
        lam = jnp.exp(jnp.sum(lq[0] * lk[0])) - jnp.exp(jnp.sum(lq[1] * lk[1])) + lam_init
        a = _diff_attention(q, k, v, lam, g_subln[l], lam_init) @ w_attn_proj[l]
        c = _conformer_conv(zu, w_dw[l], b_dw[l], g_conv_ln[l], b_conv_ln[l],
                            w_conv_proj[l], b_conv_proj[l])
        gates = jax.nn.sigmoid(zg + b_gate[l]).reshape(b, s, N_BRANCH, D_MODEL)
        merged = gates[:, :, 0] * a + gates[:, :, 1] * c
        x = x + merged @ w_out[l]
        h2 = _rmsnorm(x, g_ffn[l])
        f_gate, f_up = jnp.split(h2 @ w_ffn_in[l], 2, axis=-1)
        x = x + (jax.nn.silu(f_gate) * f_up) @ w_ffn_out[l]
    return _rmsnorm(x, g_final)


def setup_inputs(seed: int = 0) -> dict:
    key = jax.random.key(seed)
    ks = jax.random.split(key, 20)

    def nrm(k, shape, scale):
        return jax.random.normal(k, shape, dtype=jnp.float32) * scale

    def gain(k, shape):
        return 1.0 + nrm(k, shape, 0.02)

    res_scale = (2 * DEPTH) ** -0.5
    return {
        "x_prompt": nrm(ks[0], (BATCH, SEQ, D_MODEL), 1.0),
        "x_sample": nrm(ks[1], (DEC_BATCH, DEC_SEQ, D_MODEL), 1.0),
        "g_mix": gain(ks[2], (DEPTH, D_MODEL)),
        "w_in": nrm(ks[3], (DEPTH, D_MODEL, IN_WIDTH), D_MODEL ** -0.5),
        "b_gate": nrm(ks[4], (DEPTH, N_BRANCH * D_MODEL), 0.02),
        "lambda_q": nrm(ks[5], (DEPTH, 2, HEAD_DIM), 0.1),
        "lambda_k": nrm(ks[6], (DEPTH, 2, HEAD_DIM), 0.1),
        "g_subln": gain(ks[7], (DEPTH, V_DIM)),
        "w_attn_proj": nrm(ks[8], (DEPTH, V_WIDTH, D_MODEL), V_WIDTH ** -0.5),
        "w_dw": nrm(ks[9], (DEPTH, CONV_WIDTH, 1, CONV_CH), CONV_WIDTH ** -0.5),
        "b_dw": nrm(ks[10], (DEPTH, CONV_CH), 0.02),
        "g_conv_ln": gain(ks[11], (DEPTH, CONV_CH)),
        "b_conv_ln": nrm(ks[12], (DEPTH, CONV_CH), 0.02),
        "w_conv_proj": nrm(ks[13], (DEPTH, CONV_CH, D_MODEL), CONV_CH ** -0.5),
        "b_conv_proj": nrm(ks[14], (DEPTH, D_MODEL), 0.02),
        "w_out": nrm(ks[15], (DEPTH, D_MODEL, D_MODEL), D_MODEL ** -0.5 * res_scale),
        "g_ffn": gain(ks[16], (DEPTH, D_MODEL)),
        "w_ffn_in": nrm(ks[17], (DEPTH, D_MODEL, 2 * D_FF), D_MODEL ** -0.5),
        "w_ffn_out": nrm(ks[18], (DEPTH, D_FF, D_MODEL), D_FF ** -0.5 * res_scale),
        "g_final": gain(ks[19], (D_MODEL,)),
    }


def reference(x_prompt, x_sample, g_mix, w_in, b_gate, lambda_q, lambda_k, g_subln,
              w_attn_proj, w_dw, b_dw, g_conv_ln, b_conv_ln, w_conv_proj, b_conv_proj,
              w_out, g_ffn, w_ffn_in, w_ffn_out, g_final):
    y_prompt = _trunk(x_prompt, g_mix, w_in, b_gate, lambda_q, lambda_k, g_subln,
                      w_attn_proj, w_dw, b_dw, g_conv_ln, b_conv_ln, w_conv_proj,
                      b_conv_proj, w_out, g_ffn, w_ffn_in, w_ffn_out, g_final)
    y_sample = _trunk(x_sample, g_mix, w_in, b_gate, lambda_q, lambda_k, g_subln,
                      w_attn_proj, w_dw, b_dw, g_conv_ln, b_conv_ln, w_conv_proj,
                      b_conv_proj, w_out, g_ffn, w_ffn_in, w_ffn_out, g_final)
    return (y_prompt, y_sample)
```

```python
import functools

import numpy as np
import jax
import jax.numpy as jnp
from jax import lax
from jax.experimental import pallas as pl
from jax.experimental.pallas import tpu as pltpu

NORM_EPS = 1e-6
LN_EPS = 1e-5
HALO = 16
VMEM_LIMIT = 52 * 1024 * 1024

F32 = jnp.float32
BF16 = jnp.bfloat16


def _params(n_axes):
    sem = ("parallel",) + ("arbitrary",) * (n_axes - 1)
    return pltpu.CompilerParams(dimension_semantics=sem, vmem_limit_bytes=VMEM_LIMIT)


def _tile(n, want):
    t = min(n, want)
    assert n % t == 0, (n, want)
    return t


def _rmsnorm_to(x_ref, g_ref, h_ref):
    xf = x_ref[...]
    y = xf * lax.rsqrt(jnp.mean(xf * xf, axis=-1, keepdims=True) + NORM_EPS)
    h_ref[...] = (y * g_ref[...]).astype(h_ref.dtype)


def _qkv_kernel(x_ref, g_ref, w_ref, o_ref, h_ref):
    @pl.when(pl.program_id(1) == 0)
    def _():
        _rmsnorm_to(x_ref, g_ref, h_ref)

    o_ref[...] = jnp.dot(h_ref[...], w_ref[...], preferred_element_type=F32).astype(o_ref.dtype)


def _glu_kernel(x_ref, g_ref, wv_ref, wg_ref, o_ref, h_ref):
    @pl.when(pl.program_id(1) == 0)
    def _():
        _rmsnorm_to(x_ref, g_ref, h_ref)

    h = h_ref[...]
    val = jnp.dot(h, wv_ref[...], preferred_element_type=F32)
    gate = jnp.dot(h, wg_ref[...], preferred_element_type=F32)
    o_ref[...] = (val * jax.nn.sigmoid(gate)).astype(o_ref.dtype)


def _gates_kernel(x_ref, g_ref, w_ref, b_ref, o_ref, h_ref):
    @pl.when(pl.program_id(1) == 0)
    def _():
        _rmsnorm_to(x_ref, g_ref, h_ref)

    z = jnp.dot(h_ref[...], w_ref[...], preferred_element_type=F32)
    o_ref[...] = jax.nn.sigmoid(z + b_ref[...]).astype(o_ref.dtype)


def _in_proj(x, g_mix3, w_in, b_gate3, l, widths, tm, tn):
    t, d = x.shape
    qkv_w, conv_ch, gate_w = widths
    tm = _tile(t, tm)
    x_spec = pl.BlockSpec((tm, d), lambda i, j: (i, 0))
    g_spec = pl.BlockSpec((None, 1, d), lambda i, j: (l, 0, 0))
    h_scr = pltpu.VMEM((tm, d), BF16)

    def w_spec(width, col0):
        assert col0 % width == 0
        return pl.BlockSpec((None, d, width), lambda i, j: (l, 0, col0 // width + j))

    tq = _tile(qkv_w, tn)
    qkv = pl.pallas_call(
        _qkv_kernel,
        out_shape=jax.ShapeDtypeStruct((t, qkv_w), BF16),
        grid=(t // tm, qkv_w // tq),
        in_specs=[x_spec, g_spec, w_spec(tq, 0)],
        out_specs=pl.BlockSpec((tm, tq), lambda i, j: (i, j)),
        scratch_shapes=[h_scr],
        compiler_params=_params(2),
        name="in_proj_qkv",
    )(x, g_mix3, w_in)

    tc = _tile(conv_ch, tn // 2)
    glu = pl.pallas_call(
        _glu_kernel,
        out_shape=jax.ShapeDtypeStruct((t, conv_ch), F32),
        grid=(t // tm, conv_ch // tc),
        in_specs=[x_spec, g_spec, w_spec(tc, qkv_w), w_spec(tc, qkv_w + conv_ch)],
        out_specs=pl.BlockSpec((tm, tc), lambda i, j: (i, j)),
        scratch_shapes=[h_scr],
        compiler_params=_params(2),
        name="in_proj_glu",
    )(x, g_mix3, w_in, w_in)

    tg = _tile(gate_w, tn)
    gates = pl.pallas_call(
        _gates_kernel,
        out_shape=jax.ShapeDtypeStruct((t, gate_w), F32),
        grid=(t // tm, gate_w // tg),
        in_specs=[x_spec, g_spec, w_spec(tg, qkv_w + 2 * conv_ch),
                  pl.BlockSpec((None, 1, tg), lambda i, j: (l, 0, j))],
        out_specs=pl.BlockSpec((tm, tg), lambda i, j: (i, j)),
        scratch_shapes=[h_scr],
        compiler_params=_params(2),
        name="in_proj_gates",
    )(x, g_mix3, w_in, b_gate3)
    return qkv, glu, gates


def _attn_kernel(slopes_ref, lq_ref, lk_ref, gs_ref, q_ref, k_ref, v_ref, o_ref, *,
                 lam_init, head_dim, tq):
    hd = head_dim
    h = pl.program_id(1)
    qi = pl.program_id(2)
    s_len = k_ref.shape[0]
    scale = hd ** -0.5

    lq = lq_ref[...]
    lk = lk_ref[...]
    lam = (jnp.exp(jnp.sum(lq[0:1] * lk[0:1], axis=-1, keepdims=True))
           - jnp.exp(jnp.sum(lq[1:2] * lk[1:2], axis=-1, keepdims=True)) + lam_init)

    qpos = qi * tq + lax.broadcasted_iota(jnp.int32, (tq, s_len), 0)
    kpos = lax.broadcasted_iota(jnp.int32, (tq, s_len), 1)
    dist = jnp.abs(qpos - kpos).astype(F32)
    bias = (-slopes_ref[h]) * dist

    def probs(m):
        sc = lax.dot_general(q_ref[:, m * hd:(m + 1) * hd], k_ref[:, m * hd:(m + 1) * hd],
                             (((1,), (1,)), ((), ())), preferred_element_type=F32)
        sc = sc * scale + bias
        p = jnp.exp(sc - jnp.max(sc, axis=-1, keepdims=True))
        return p, jnp.sum(p, axis=-1, keepdims=True)

    p0, l0 = probs(0)
    p1, l1 = probs(1)
    w = p0 * (1.0 / l0) - p1 * (lam / l1)
    o = jnp.dot(w.astype(v_ref.dtype), v_ref[...], preferred_element_type=F32)
    y = o * lax.rsqrt(jnp.mean(o * o, axis=-1, keepdims=True) + NORM_EPS)
    o_ref[...] = ((y * gs_ref[...]) * (1.0 - lam_init)).astype(o_ref.dtype)


def _attention(qkv, slopes, lambda_q, lambda_k, g_subln3, l, lam_init, row0, n_seq, s_len, n_heads,
               head_dim, tq):
    v_dim = 2 * head_dim
    tq = _tile(s_len, tq)
    assert row0 % s_len == 0
    seq0 = row0 // s_len
    qb0 = row0 // tq
    nq = s_len // tq
    kern = functools.partial(_attn_kernel, lam_init=lam_init, head_dim=head_dim, tq=tq)
    return pl.pallas_call(
        kern,
        out_shape=jax.ShapeDtypeStruct((n_seq * s_len, n_heads * v_dim), BF16),
        grid=(n_seq, n_heads, nq),
        in_specs=[
            pl.BlockSpec(memory_space=pltpu.SMEM),
            pl.BlockSpec((None, 2, head_dim), lambda b, h, i: (l, 0, 0)),
            pl.BlockSpec((None, 2, head_dim), lambda b, h, i: (l, 0, 0)),
            pl.BlockSpec((None, 1, v_dim), lambda b, h, i: (l, 0, 0)),
            pl.BlockSpec((tq, v_dim), lambda b, h, i: (qb0 + b * nq + i, h)),
            pl.BlockSpec((s_len, v_dim), lambda b, h, i: (seq0 + b, n_heads + h)),
            pl.BlockSpec((s_len, v_dim), lambda b, h, i: (seq0 + b, 2 * n_heads + h)),
        ],
        out_specs=pl.BlockSpec((tq, v_dim), lambda b, h, i: (b * nq + i, h)),
        compiler_params=_params(3),
        name="diff_attention",
    )(slopes, lambda_q, lambda_k, g_subln3, qkv, qkv, qkv)


def _conv_kernel(edge_ref, prev_ref, main_ref, next_ref, w_ref, bdw_ref, gln_ref, bln_ref, o_ref, win_ref,
                 acc_ref, *, tt, n_taps, rows, lanes):
    i = pl.program_id(0)
    at_start = edge_ref[2 * i] != 0
    at_end = edge_ref[2 * i + 1] != 0
    win_ref[0:HALO, :] = jnp.where(at_start, 0.0, prev_ref[...])
    win_ref[HALO:HALO + tt, :] = main_ref[...]
    win_ref[HALO + tt:, :] = jnp.where(at_end, 0.0, next_ref[...])

    pad = (n_taps - 1) // 2
    c = main_ref.shape[1]
    for c0 in range(0, c, lanes):
        wts = [w_ref[k:k + 1, c0:c0 + lanes] for k in range(n_taps)]
        for r0 in range(0, tt, rows):
            acc = None
            for k in range(n_taps):
                start = HALO + r0 + k - pad
                term = wts[k] * win_ref[start:start + rows, c0:c0 + lanes]
                acc = term if acc is None else acc + term
            acc_ref[r0:r0 + rows, c0:c0 + lanes] = acc + bdw_ref[:, c0:c0 + lanes]

    y = acc_ref[...]
    yc = y - jnp.mean(y, axis=-1, keepdims=True)
    yn = yc * lax.rsqrt(jnp.mean(yc * yc, axis=-1, keepdims=True) + LN_EPS)
    z = yn * gln_ref[...] + bln_ref[...]
    o_ref[...] = (z * jax.nn.sigmoid(z)).astype(o_ref.dtype)


def _conv_module(glu, w_dw, b_dw3, g_ln3, b_ln3, l, seq_bounds, tt):
    t, c = glu.shape
    n_taps = w_dw.shape[1]
    assert (n_taps - 1) // 2 <= HALO
    tt = _tile(min(seq_bounds[1], seq_bounds[2]), tt)
    assert tt % HALO == 0 and t % tt == 0
    hb = tt // HALO
    n_hblk = t // HALO
    vec = pl.BlockSpec((None, 1, c), lambda i: (l, 0, 0))
    t_first, s_first, s_second = seq_bounds
    starts = np.arange(0, t, tt)
    s_of = np.where(starts < t_first, s_first, s_second)
    rel = np.where(starts < t_first, starts, starts - t_first)
    edges = np.stack([rel % s_of == 0, (rel + tt) % s_of == 0], axis=1).astype(np.int32).reshape(-1)
    kern = functools.partial(_conv_kernel, tt=tt, n_taps=n_taps, rows=32, lanes=256)
    return pl.pallas_call(
        kern,
        out_shape=jax.ShapeDtypeStruct((t, c), BF16),
        grid=(t // tt,),
        in_specs=[
            pl.BlockSpec(memory_space=pltpu.SMEM),
            pl.BlockSpec((HALO, c), lambda i: (jnp.maximum(i * hb - 1, 0), 0)),
            pl.BlockSpec((tt, c), lambda i: (i, 0)),
            pl.BlockSpec((HALO, c), lambda i: (jnp.minimum((i + 1) * hb, n_hblk - 1), 0)),
            pl.BlockSpec((None, n_taps, c), lambda i: (l, 0, 0)),
            vec, vec, vec,
        ],
        out_specs=pl.BlockSpec((tt, c), lambda i: (i, 0)),
        scratch_shapes=[pltpu.VMEM((tt + 2 * HALO, c), F32), pltpu.VMEM((tt, c), F32)],
        compiler_params=_params(1),
        name="conv_module",
    )(jnp.asarray(edges), glu, glu, glu, w_dw, b_dw3, g_ln3, b_ln3)


def _merge_kernel(a_ref, c_ref, wa_ref, wc_ref, bc_ref, ga_ref, gc_ref, o_ref):
    a = jnp.dot(a_ref[...], wa_ref[...], preferred_element_type=F32)
    c = jnp.dot(c_ref[...], wc_ref[...], preferred_element_type=F32) + bc_ref[...]
    o_ref[...] = (ga_ref[...] * a + gc_ref[...] * c).astype(o_ref.dtype)


def _merge(attn, conv, w_attn_proj, w_conv_proj, b_conv_proj3, gates, l, tm, tn):
    t, kv = attn.shape
    kc = conv.shape[1]
    d = w_attn_proj.shape[2]
    tm = _tile(t, tm)
    tn = _tile(d, tn)
    nj = d // tn
    return pl.pallas_call(
        _merge_kernel,
        out_shape=jax.ShapeDtypeStruct((t, d), BF16),
        grid=(t // tm, nj),
        in_specs=[
            pl.BlockSpec((tm, kv), lambda i, j: (i, 0)),
            pl.BlockSpec((tm, kc), lambda i, j: (i, 0)),
            pl.BlockSpec((None, kv, tn), lambda i, j: (l, 0, j)),
            pl.BlockSpec((None, kc, tn), lambda i, j: (l, 0, j)),
            pl.BlockSpec((None, 1, tn), lambda i, j: (l, 0, j)),
            pl.BlockSpec((tm, tn), lambda i, j: (i, j)),
            pl.BlockSpec((tm, tn), lambda i, j: (i, nj + j)),
        ],
        out_specs=pl.BlockSpec((tm, tn), lambda i, j: (i, j)),
        compiler_params=_params(2),
        name="branch_merge",
    )(attn, conv, w_attn_proj, w_conv_proj, b_conv_proj3, gates, gates)


def _residual_matmul_kernel(a_ref, w_ref, x_ref, o_ref):
    o_ref[...] = x_ref[...] + jnp.dot(a_ref[...], w_ref[...], preferred_element_type=F32)


def _residual_matmul(a, w, x, l, tm, tn, name):
    t, k = a.shape
    d = w.shape[2]
    tm = _tile(t, tm)
    tn = _tile(d, tn)
    return pl.pallas_call(
        _residual_matmul_kernel,
        out_shape=jax.ShapeDtypeStruct((t, d), F32),
        grid=(t // tm, d // tn),
        in_specs=[
            pl.BlockSpec((tm, k), lambda i, j: (i, 0)),
            pl.BlockSpec((None, k, tn), lambda i, j: (l, 0, j)),
            pl.BlockSpec((tm, tn), lambda i, j: (i, j)),
        ],
        out_specs=pl.BlockSpec((tm, tn), lambda i, j: (i, j)),
        compiler_params=_params(2),
        name=name,
    )(a, w, x)


def _ffn_in_kernel(x_ref, g_ref, wg_ref, wu_ref, o_ref, h_ref):
    @pl.when(pl.program_id(1) == 0)
    def _():
        _rmsnorm_to(x_ref, g_ref, h_ref)

    h = h_ref[...]
    gate = jnp.dot(h, wg_ref[...], preferred_element_type=F32)
    up = jnp.dot(h, wu_ref[...], preferred_element_type=F32)
    o_ref[...] = ((gate * jax.nn.sigmoid(gate)) * up).astype(o_ref.dtype)


def _ffn_in(x, g_ffn3, w_ffn_in, l, tm, tn):
    t, d = x.shape
    d_ff = w_ffn_in.shape[2] // 2
    tm = _tile(t, tm)
    tn = _tile(d_ff, tn)
    nj = d_ff // tn
    return pl.pallas_call(
        _ffn_in_kernel,
        out_shape=jax.ShapeDtypeStruct((t, d_ff), BF16),
        grid=(t // tm, nj),
        in_specs=[
            pl.BlockSpec((tm, d), lambda i, j: (i, 0)),
            pl.BlockSpec((None, 1, d), lambda i, j: (l, 0, 0)),
            pl.BlockSpec((None, d, tn), lambda i, j: (l, 0, j)),
            pl.BlockSpec((None, d, tn), lambda i, j: (l, 0, nj + j)),
        ],
        out_specs=pl.BlockSpec((tm, tn), lambda i, j: (i, j)),
        scratch_shapes=[pltpu.VMEM((tm, d), BF16)],
        compiler_params=_params(2),
        name="ffn_in",
    )(x, g_ffn3, w_ffn_in, w_ffn_in)


def _final_norm_kernel(x_ref, g_ref, o_ref):
    _rmsnorm_to(x_ref, g_ref, o_ref)


def _final_norm(x, g2, row0, n_rows, tm):
    d = x.shape[1]
    tm = _tile(n_rows, tm)
    assert row0 % tm == 0
    blk0 = row0 // tm
    return pl.pallas_call(
        _final_norm_kernel,
        out_shape=jax.ShapeDtypeStruct((n_rows, d), F32),
        grid=(n_rows // tm,),
        in_specs=[pl.BlockSpec((tm, d), lambda i: (blk0 + i, 0)),
                  pl.BlockSpec((1, d), lambda i: (0, 0))],
        out_specs=pl.BlockSpec((tm, d), lambda i: (i, 0)),
        compiler_params=_params(1),
        name="final_norm",
    )(x, g2)


def _lambda_init(layer):
    return 0.8 - 0.6 * float(np.exp(-0.3 * layer))


def kernel(x_prompt, x_sample, g_mix, w_in, b_gate, lambda_q, lambda_k, g_subln, w_attn_proj, w_dw, b_dw,
           g_conv_ln, b_conv_ln, w_conv_proj, b_conv_proj, w_out, g_ffn, w_ffn_in, w_ffn_out, g_final):
    b1, s1, d = x_prompt.shape
    b2, s2, _ = x_sample.shape
    depth = w_in.shape[0]
    head_dim = lambda_q.shape[2]
    v_width = w_attn_proj.shape[1]
    n_heads = v_width // (2 * head_dim)
    qk_width = n_heads * 2 * head_dim
    conv_ch = w_conv_proj.shape[1]
    gate_w = b_gate.shape[1]
    assert w_in.shape[2] == 2 * qk_width + v_width + 2 * conv_ch + gate_w and gate_w == 2 * d
    t1, t2 = b1 * s1, b2 * s2

    x = jnp.concatenate([x_prompt.reshape(t1, d), x_sample.reshape(t2, d)], axis=0)
    slopes = jnp.asarray(2.0 ** (-8.0 * np.arange(1, n_heads + 1) / n_heads), dtype=F32)

    def row(p):
        return p.reshape(p.shape[0], 1, p.shape[1])

    w_in_b = w_in.astype(BF16)
    w_attn_b = w_attn_proj.astype(BF16)
    w_conv_b = w_conv_proj.astype(BF16)
    w_out_b = w_out.astype(BF16)
    w_ffn_in_b = w_ffn_in.astype(BF16)
    w_ffn_out_b = w_ffn_out.astype(BF16)
    w_dw3 = w_dw.reshape(depth, w_dw.shape[1], conv_ch)
    g_mix3, b_gate3, g_subln3, b_dw3 = row(g_mix), row(b_gate), row(g_subln), row(b_dw)
    g_ln3, b_ln3, b_pw3, g_ffn3 = row(g_conv_ln), row(b_conv_ln), row(b_conv_proj), row(g_ffn)

    for l in range(depth):
        lam_init = _lambda_init(l)
        qkv, glu, gates = _in_proj(x, g_mix3, w_in_b, b_gate3, l,
                                   (2 * qk_width + v_width, conv_ch, gate_w), tm=1024, tn=1024)
        attn = jnp.concatenate([
            _attention(qkv, slopes, lambda_q, lambda_k, g_subln3, l, lam_init, 0, b1, s1, n_heads, head_dim, tq=256),
            _attention(qkv, slopes, lambda_q, lambda_k, g_subln3, l, lam_init, t1, b2, s2, n_heads, head_dim, tq=256),
        ], axis=0)
        conv = _conv_module(glu, w_dw3, b_dw3, g_ln3, b_ln3, l, (t1, s1, s2), tt=128)
        merged = _merge(attn, conv, w_attn_b, w_conv_b, b_pw3, gates, l, tm=1024, tn=512)
        x = _residual_matmul(merged, w_out_b, x, l, tm=1024, tn=1024, name="out_proj")
        act = _ffn_in(x, g_ffn3, w_ffn_in_b, l, tm=1024, tn=512)
        x = _residual_matmul(act, w_ffn_out_b, x, l, tm=512, tn=512, name="ffn_out")

    g2 = g_final.reshape(1, d)
    y1 = _final_norm(x, g2, 0, t1, tm=512).reshape(b1, s1, d)
    y2 = _final_norm(x, g2, t1, t2, tm=512).reshape(b2, s2, d)
    return (y1, y2)
```

```python
import functools

import numpy as np
import jax
import jax.numpy as jnp
from jax import lax
from jax.experimental import pallas as pl
from jax.experimental.pallas import tpu as pltpu

NORM_EPS = 1e-6
LN_EPS = 1e-5
SUBLANES = 8
LANES = 128
HALO = 16
VMEM_LIMIT = 52 * 1024 * 1024

F32 = jnp.float32
BF16 = jnp.bfloat16


def _params(n_axes):
    sem = ("parallel",) + ("arbitrary",) * (n_axes - 1)
    return pltpu.CompilerParams(dimension_semantics=sem, vmem_limit_bytes=VMEM_LIMIT)


def _tile(n, want):
    t = min(n, want)
    assert n % t == 0, (n, want)
    return t


def _rmsnorm_to(x_ref, g_ref, h_ref):
    xf = x_ref[...]
    y = xf * lax.rsqrt(jnp.mean(xf * xf, axis=-1, keepdims=True) + NORM_EPS)
    h_ref[...] = (y * g_ref[...]).astype(h_ref.dtype)


def _qkv_kernel(x_ref, g_ref, w_ref, o_ref, h_ref):
    @pl.when(pl.program_id(1) == 0)
    def _():
        _rmsnorm_to(x_ref, g_ref, h_ref)

    o_ref[...] = jnp.dot(h_ref[...], w_ref[...], preferred_element_type=F32).astype(o_ref.dtype)


def _glu_kernel(x_ref, g_ref, wv_ref, wg_ref, o_ref, h_ref):
    @pl.when(pl.program_id(1) == 0)
    def _():
        _rmsnorm_to(x_ref, g_ref, h_ref)

    h = h_ref[...]
    val = jnp.dot(h, wv_ref[...], preferred_element_type=F32)
    gate = jnp.dot(h, wg_ref[...], preferred_element_type=F32)
    o_ref[...] = (val * jax.nn.sigmoid(gate)).astype(o_ref.dtype)


def _gates_kernel(x_ref, g_ref, w_ref, b_ref, o_ref, h_ref):
    @pl.when(pl.program_id(1) == 0)
    def _():
        _rmsnorm_to(x_ref, g_ref, h_ref)

    z = jnp.dot(h_ref[...], w_ref[...], preferred_element_type=F32)
    o_ref[...] = jax.nn.sigmoid(z + b_ref[...]).astype(o_ref.dtype)


def _in_proj(x, g_mix3, w_in, b_gate3, l, widths, tm, tn):
    t, d = x.shape
    qkv_w, conv_ch, gate_w = widths
    tm = _tile(t, tm)
    x_spec = pl.BlockSpec((tm, d), lambda i, j: (i, 0))
    g_spec = pl.BlockSpec((None, 1, d), lambda i, j: (l, 0, 0))
    h_scr = pltpu.VMEM((tm, d), BF16)

    def w_spec(width, col0):
        assert col0 % width == 0
        return pl.BlockSpec((None, d, width), lambda i, j: (l, 0, col0 // width + j))

    tq = _tile(qkv_w, tn)
    qkv = pl.pallas_call(
        _qkv_kernel,
        out_shape=jax.ShapeDtypeStruct((t, qkv_w), BF16),
        grid=(t // tm, qkv_w // tq),
        in_specs=[x_spec, g_spec, w_spec(tq, 0)],
        out_specs=pl.BlockSpec((tm, tq), lambda i, j: (i, j)),
        scratch_shapes=[h_scr],
        compiler_params=_params(2),
        name="in_proj_qkv",
    )(x, g_mix3, w_in)

    tc = _tile(conv_ch, tn // 2)
    glu = pl.pallas_call(
        _glu_kernel,
        out_shape=jax.ShapeDtypeStruct((t, conv_ch), F32),
        grid=(t // tm, conv_ch // tc),
        in_specs=[x_spec, g_spec, w_spec(tc, qkv_w), w_spec(tc, qkv_w + conv_ch)],
        out_specs=pl.BlockSpec((tm, tc), lambda i, j: (i, j)),
        scratch_shapes=[h_scr],
        compiler_params=_params(2),
        name="in_proj_glu",
    )(x, g_mix3, w_in, w_in)

    tg = _tile(gate_w, tn)
    gates = pl.pallas_call(
        _gates_kernel,
        out_shape=jax.ShapeDtypeStruct((t, gate_w), F32),
        grid=(t // tm, gate_w // tg),
        in_specs=[x_spec, g_spec, w_spec(tg, qkv_w + 2 * conv_ch),
                  pl.BlockSpec((None, 1, tg), lambda i, j: (l, 0, j))],
        out_specs=pl.BlockSpec((tm, tg), lambda i, j: (i, j)),
        scratch_shapes=[h_scr],
        compiler_params=_params(2),
        name="in_proj_gates",
    )(x, g_mix3, w_in, b_gate3)
    return qkv, glu, gates


def _attn_kernel(slopes_ref, lq_ref, lk_ref, gs_ref, q_ref, k_ref, v_ref, o_ref, *,
                 lam_init, head_dim, tq):
    hd = head_dim
    h = pl.program_id(1)
    qi = pl.program_id(2)
    s_len = k_ref.shape[0]
    scale = hd ** -0.5

    lq = lq_ref[...]
    lk = lk_ref[...]
    lam = (jnp.exp(jnp.sum(lq[0:1] * lk[0:1], axis=-1, keepdims=True))
           - jnp.exp(jnp.sum(lq[1:2] * lk[1:2], axis=-1, keepdims=True)) + lam_init)

    qpos = qi * tq + lax.broadcasted_iota(jnp.int32, (tq, s_len), 0)
    kpos = lax.broadcasted_iota(jnp.int32, (tq, s_len), 1)
    dist = jnp.abs(qpos - kpos).astype(F32)
    bias = (-slopes_ref[h]) * dist

    def probs(m):
        sc = lax.dot_general(q_ref[:, m * hd:(m + 1) * hd], k_ref[:, m * hd:(m + 1) * hd],
                             (((1,), (1,)), ((), ())), preferred_element_type=F32)
        sc = sc * scale + bias
        p = jnp.exp(sc - jnp.max(sc, axis=-1, keepdims=True))
        return p, jnp.sum(p, axis=-1, keepdims=True)

    p0, l0 = probs(0)
    p1, l1 = probs(1)
    w = p0 * (1.0 / l0) - p1 * (lam / l1)
    o = jnp.dot(w.astype(v_ref.dtype), v_ref[...], preferred_element_type=F32)
    y = o * lax.rsqrt(jnp.mean(o * o, axis=-1, keepdims=True) + NORM_EPS)
    o_ref[...] = ((y * gs_ref[...]) * (1.0 - lam_init)).astype(o_ref.dtype)


def _attention(qkv, slopes, lambda_q, lambda_k, g_subln3, l, lam_init, n_seq, s_len, n_heads, head_dim, tq):
    v_dim = 2 * head_dim
    tq = _tile(s_len, tq)
    nq = s_len // tq
    kern = functools.partial(_attn_kernel, lam_init=lam_init, head_dim=head_dim, tq=tq)
    return pl.pallas_call(
        kern,
        out_shape=jax.ShapeDtypeStruct((n_seq * s_len, n_heads * v_dim), BF16),
        grid=(n_seq, n_heads, nq),
        in_specs=[
            pl.BlockSpec(memory_space=pltpu.SMEM),
            pl.BlockSpec((None, 2, head_dim), lambda b, h, i: (l, 0, 0)),
            pl.BlockSpec((None, 2, head_dim), lambda b, h, i: (l, 0, 0)),
            pl.BlockSpec((None, 1, v_dim), lambda b, h, i: (l, 0, 0)),
            pl.BlockSpec((tq, v_dim), lambda b, h, i: (b * nq + i, h)),
            pl.BlockSpec((s_len, v_dim), lambda b, h, i: (b, n_heads + h)),
            pl.BlockSpec((s_len, v_dim), lambda b, h, i: (b, 2 * n_heads + h)),
        ],
        out_specs=pl.BlockSpec((tq, v_dim), lambda b, h, i: (b * nq + i, h)),
        compiler_params=_params(3),
        name="diff_attention",
    )(slopes, lambda_q, lambda_k, g_subln3, qkv, qkv, qkv)


def _conv_kernel(prev_ref, main_ref, next_ref, w_ref, bdw_ref, gln_ref, bln_ref, o_ref, win_ref, sh_ref, acc_ref,
                 *, tt, n_taps, tiles_per_seq, rows, lanes):
    pos = pl.program_id(0) % tiles_per_seq
    win_ref[0:HALO, :] = jnp.where(pos == 0, 0.0, prev_ref[...])
    win_ref[HALO:HALO + tt, :] = main_ref[...]
    win_ref[HALO + tt:, :] = jnp.where(pos == tiles_per_seq - 1, 0.0, next_ref[...])

    span = sh_ref.shape[1]
    for s in range(1, SUBLANES):
        sh_ref[s - 1] = win_ref[s:s + span, :]

    first = HALO - (n_taps - 1) // 2
    def lane_chunk(ci, carry):
        cols = pl.ds(pl.multiple_of(ci * lanes, lanes), lanes)
        for r0 in range(0, tt, rows):
            acc = None
            for s in range(SUBLANES):
                offs = [d for d in range(first, first + n_taps) if d % SUBLANES == s]
                if not offs:
                    continue
                lo, hi = offs[0] - s, offs[-1] - s
                span_rows = slice(r0 + lo, r0 + hi + rows)
                slab = win_ref[span_rows, cols] if s == 0 else sh_ref[s - 1, span_rows, cols]
                for d in offs:
                    k = d - first
                    term = w_ref[k:k + 1, cols] * slab[d - s - lo:d - s - lo + rows]
                    acc = term if acc is None else acc + term
            acc_ref[r0:r0 + rows, cols] = acc + bdw_ref[:, cols]
        return carry

    lax.fori_loop(0, main_ref.shape[1] // lanes, lane_chunk, 0)

    y = acc_ref[...]
    yc = y - jnp.mean(y, axis=-1, keepdims=True)
    yn = yc * lax.rsqrt(jnp.mean(yc * yc, axis=-1, keepdims=True) + LN_EPS)
    z = yn * gln_ref[...] + bln_ref[...]
    o_ref[...] = (z * jax.nn.sigmoid(z)).astype(o_ref.dtype)


def _conv_module(glu, w_dw, b_dw3, g_ln3, b_ln3, l, s_len, tt):
    t, c = glu.shape
    n_taps = w_dw.shape[1]
    pad = (n_taps - 1) // 2
    assert pad <= HALO
    tt = _tile(s_len, tt)
    assert tt % HALO == 0 and t % s_len == 0
    hb = tt // HALO
    n_hblk = t // HALO
    span = tt + (HALO + pad) // SUBLANES * SUBLANES
    vec = pl.BlockSpec((None, 1, c), lambda i: (l, 0, 0))
    kern = functools.partial(_conv_kernel, tt=tt, n_taps=n_taps, tiles_per_seq=s_len // tt, rows=64, lanes=LANES)
    return pl.pallas_call(
        kern,
        out_shape=jax.ShapeDtypeStruct((t, c), BF16),
        grid=(t // tt,),
        in_specs=[
            pl.BlockSpec((HALO, c), lambda i: (jnp.maximum(i * hb - 1, 0), 0)),
            pl.BlockSpec((tt, c), lambda i: (i, 0)),
            pl.BlockSpec((HALO, c), lambda i: (jnp.minimum((i + 1) * hb, n_hblk - 1), 0)),
            pl.BlockSpec((None, n_taps, c), lambda i: (l, 0, 0)),
            vec, vec, vec,
        ],
        out_specs=pl.BlockSpec((tt, c), lambda i: (i, 0)),
        scratch_shapes=[pltpu.VMEM((tt + 2 * HALO, c), F32),
                        pltpu.VMEM((SUBLANES - 1, span, c), F32),
                        pltpu.VMEM((tt, c), F32)],
        compiler_params=_params(1),
        name="conv_module",
    )(glu, glu, glu, w_dw, b_dw3, g_ln3, b_ln3)


def _merge_kernel(a_ref, c_ref, wa_ref, wc_ref, bc_ref, ga_ref, gc_ref, o_ref):
    a = jnp.dot(a_ref[...], wa_ref[...], preferred_element_type=F32)
    c = jnp.dot(c_ref[...], wc_ref[...], preferred_element_type=F32) + bc_ref[...]
    o_ref[...] = (ga_ref[...] * a + gc_ref[...] * c).astype(o_ref.dtype)


def _merge(attn, conv, w_attn_proj, w_conv_proj, b_conv_proj3, gates, l, tm, tn):
    t, kv = attn.shape
    kc = conv.shape[1]
    d = w_attn_proj.shape[2]
    tm = _tile(t, tm)
    tn = _tile(d, tn)
    nj = d // tn
    return pl.pallas_call(
        _merge_kernel,
        out_shape=jax.ShapeDtypeStruct((t, d), BF16),
        grid=(t // tm, nj),
        in_specs=[
            pl.BlockSpec((tm, kv), lambda i, j: (i, 0)),
            pl.BlockSpec((tm, kc), lambda i, j: (i, 0)),
            pl.BlockSpec((None, kv, tn), lambda i, j: (l, 0, j)),
            pl.BlockSpec((None, kc, tn), lambda i, j: (l, 0, j)),
            pl.BlockSpec((None, 1, tn), lambda i, j: (l, 0, j)),
            pl.BlockSpec((tm, tn), lambda i, j: (i, j)),
            pl.BlockSpec((tm, tn), lambda i, j: (i, nj + j)),
        ],
        out_specs=pl.BlockSpec((tm, tn), lambda i, j: (i, j)),
        compiler_params=_params(2),
        name="branch_merge",
    )(attn, conv, w_attn_proj, w_conv_proj, b_conv_proj3, gates, gates)


def _residual_matmul_kernel(a_ref, w_ref, x_ref, o_ref):
    o_ref[...] = x_ref[...] + jnp.dot(a_ref[...], w_ref[...], preferred_element_type=F32)


def _residual_matmul(a, w, x, l, tm, tn, name):
    t, k = a.shape
    d = w.shape[2]
    tm = _tile(t, tm)
    tn = _tile(d, tn)
    return pl.pallas_call(
        _residual_matmul_kernel,
        out_shape=jax.ShapeDtypeStruct((t, d), F32),
        grid=(t // tm, d // tn),
        in_specs=[
            pl.BlockSpec((tm, k), lambda i, j: (i, 0)),
            pl.BlockSpec((None, k, tn), lambda i, j: (l, 0, j)),
            pl.BlockSpec((tm, tn), lambda i, j: (i, j)),
        ],
        out_specs=pl.BlockSpec((tm, tn), lambda i, j: (i, j)),
        compiler_params=_params(2),
        name=name,
    )(a, w, x)


def _ffn_in_kernel(x_ref, g_ref, wg_ref, wu_ref, o_ref, h_ref):
    @pl.when(pl.program_id(1) == 0)
    def _():
        _rmsnorm_to(x_ref, g_ref, h_ref)

    h = h_ref[...]
    gate = jnp.dot(h, wg_ref[...], preferred_element_type=F32)
    up = jnp.dot(h, wu_ref[...], preferred_element_type=F32)
    o_ref[...] = ((gate * jax.nn.sigmoid(gate)) * up).astype(o_ref.dtype)


def _ffn_in(x, g_ffn3, w_ffn_in, l, tm, tn):
    t, d = x.shape
    d_ff = w_ffn_in.shape[2] // 2
    tm = _tile(t, tm)
    tn = _tile(d_ff, tn)
    nj = d_ff // tn
    return pl.pallas_call(
        _ffn_in_kernel,
        out_shape=jax.ShapeDtypeStruct((t, d_ff), BF16),
        grid=(t // tm, nj),
        in_specs=[
            pl.BlockSpec((tm, d), lambda i, j: (i, 0)),
            pl.BlockSpec((None, 1, d), lambda i, j: (l, 0, 0)),
            pl.BlockSpec((None, d, tn), lambda i, j: (l, 0, j)),
            pl.BlockSpec((None, d, tn), lambda i, j: (l, 0, nj + j)),
        ],
        out_specs=pl.BlockSpec((tm, tn), lambda i, j: (i, j)),
        scratch_shapes=[pltpu.VMEM((tm, d), BF16)],
        compiler_params=_params(2),
        name="ffn_in",
    )(x, g_ffn3, w_ffn_in, w_ffn_in)


def _final_norm_kernel(x_ref, g_ref, o_ref):
    _rmsnorm_to(x_ref, g_ref, o_ref)


def _final_norm(x, g2, tm):
    t, d = x.shape
    tm = _tile(t, tm)
    return pl.pallas_call(
        _final_norm_kernel,
        out_shape=jax.ShapeDtypeStruct((t, d), F32),
        grid=(t // tm,),
        in_specs=[pl.BlockSpec((tm, d), lambda i: (i, 0)),
                  pl.BlockSpec((1, d), lambda i: (0, 0))],
        out_specs=pl.BlockSpec((tm, d), lambda i: (i, 0)),
        compiler_params=_params(1),
        name="final_norm",
    )(x, g2)


def _lambda_init(layer):
    return 0.8 - 0.6 * float(np.exp(-0.3 * layer))


def kernel(x_prompt, x_sample, g_mix, w_in, b_gate, lambda_q, lambda_k, g_subln, w_attn_proj, w_dw, b_dw,
           g_conv_ln, b_conv_ln, w_conv_proj, b_conv_proj, w_out, g_ffn, w_ffn_in, w_ffn_out, g_final):
    d = x_prompt.shape[2]
    depth = w_in.shape[0]
    head_dim = lambda_q.shape[2]
    v_width = w_attn_proj.shape[1]
    n_heads = v_width // (2 * head_dim)
    qk_width = n_heads * 2 * head_dim
    conv_ch = w_conv_proj.shape[1]
    gate_w = b_gate.shape[1]
    qkv_w = 2 * qk_width + v_width
    assert w_in.shape[2] == qkv_w + 2 * conv_ch + gate_w and gate_w == 2 * d

    slopes = jnp.asarray(2.0 ** (-8.0 * np.arange(1, n_heads + 1) / n_heads), dtype=F32)

    def row(p):
        return p.reshape(p.shape[0], 1, p.shape[1])

    w_in_b = w_in.astype(BF16)
    w_attn_b = w_attn_proj.astype(BF16)
    w_conv_b = w_conv_proj.astype(BF16)
    w_out_b = w_out.astype(BF16)
    w_ffn_in_b = w_ffn_in.astype(BF16)
    w_ffn_out_b = w_ffn_out.astype(BF16)
    w_dw3 = w_dw.reshape(depth, w_dw.shape[1], conv_ch)
    g_mix3, b_gate3, g_subln3, b_dw3 = row(g_mix), row(b_gate), row(g_subln), row(b_dw)
    g_ln3, b_ln3, b_pw3, g_ffn3 = row(g_conv_ln), row(b_conv_ln), row(b_conv_proj), row(g_ffn)
    g2 = g_final.reshape(1, d)

    def trunk(x3):
        n_seq, s_len, _ = x3.shape
        x = x3.reshape(n_seq * s_len, d)
        for l in range(depth):
            lam_init = _lambda_init(l)
            qkv, glu, gates = _in_proj(x, g_mix3, w_in_b, b_gate3, l, (qkv_w, conv_ch, gate_w),
                                       tm=1024, tn=1024)
            attn = _attention(qkv, slopes, lambda_q, lambda_k, g_subln3, l, lam_init, n_seq, s_len, n_heads,
                              head_dim, tq=256)
            conv = _conv_module(glu, w_dw3, b_dw3, g_ln3, b_ln3, l, s_len, tt=128)
            merged = _merge(attn, conv, w_attn_b, w_conv_b, b_pw3, gates, l, tm=1024, tn=512)
            x = _residual_matmul(merged, w_out_b, x, l, tm=1024, tn=1024, name="out_proj")
            act = _ffn_in(x, g_ffn3, w_ffn_in_b, l, tm=1024, tn=512)
            x = _residual_matmul(act, w_ffn_out_b, x, l, tm=512, tn=512, name="ffn_out")
        return _final_norm(x, g2, tm=512).reshape(x3.shape)

    return (trunk(x_prompt), trunk(x_sample))
```

```python
import functools

import numpy as np
import jax
import jax.numpy as jnp
from jax import lax
from jax.experimental import pallas as pl
from jax.experimental.pallas import tpu as pltpu

NORM_EPS = 1e-6
LN_EPS = 1e-5
LOG2E = 1.4426950408889634
SUBLANES = 8
LANES = 128
HALO = 16
VMEM_LIMIT = 52 * 1024 * 1024

F32 = jnp.float32
BF16 = jnp.bfloat16


def _params(n_axes):
    sem = ("parallel",) + ("arbitrary",) * (n_axes - 1)
    return pltpu.CompilerParams(dimension_semantics=sem, vmem_limit_bytes=VMEM_LIMIT)


def _tile(n, want):
    t = min(n, want)
    assert n % t == 0, (n, want)
    return t


def _rmsnorm_to(x_ref, g_ref, h_ref):
    xf = x_ref[...]
    y = xf * lax.rsqrt(jnp.mean(xf * xf, axis=-1, keepdims=True) + NORM_EPS)
    h_ref[...] = (y * g_ref[...]).astype(h_ref.dtype)


def _qkv_kernel(x_ref, g_ref, w_ref, cs_ref, o_ref, h_ref):
    @pl.when(pl.program_id(1) == 0)
    def _():
        _rmsnorm_to(x_ref, g_ref, h_ref)

    z = jnp.dot(h_ref[...], w_ref[...], preferred_element_type=F32)
    o_ref[...] = (z * cs_ref[...]).astype(o_ref.dtype)


def _glu_kernel(x_ref, g_ref, wv_ref, wg_ref, o_ref, h_ref):
    @pl.when(pl.program_id(1) == 0)
    def _():
        _rmsnorm_to(x_ref, g_ref, h_ref)

    h = h_ref[...]
    val = jnp.dot(h, wv_ref[...], preferred_element_type=F32)
    gate = jnp.dot(h, wg_ref[...], preferred_element_type=F32)
    o_ref[...] = (val * jax.nn.sigmoid(gate)).astype(o_ref.dtype)


def _gates_kernel(x_ref, g_ref, w_ref, b_ref, o_ref, h_ref):
    @pl.when(pl.program_id(1) == 0)
    def _():
        _rmsnorm_to(x_ref, g_ref, h_ref)

    z = jnp.dot(h_ref[...], w_ref[...], preferred_element_type=F32)
    o_ref[...] = jax.nn.sigmoid(z + b_ref[...]).astype(o_ref.dtype)


def _in_proj(x, g_mix3, w_in, b_gate3, qkv_scale, l, widths, tm, tn):
    t, d = x.shape
    qkv_w, conv_ch, gate_w = widths
    tm = _tile(t, tm)
    x_spec = pl.BlockSpec((tm, d), lambda i, j: (i, 0))
    g_spec = pl.BlockSpec((None, 1, d), lambda i, j: (l, 0, 0))
    h_scr = pltpu.VMEM((tm, d), BF16)

    def w_spec(width, col0):
        assert col0 % width == 0
        return pl.BlockSpec((None, d, width), lambda i, j: (l, 0, col0 // width + j))

    tq = _tile(qkv_w, tn)
    qkv = pl.pallas_call(
        _qkv_kernel,
        out_shape=jax.ShapeDtypeStruct((t, qkv_w), BF16),
        grid=(t // tm, qkv_w // tq),
        in_specs=[x_spec, g_spec, w_spec(tq, 0), pl.BlockSpec((1, tq), lambda i, j: (0, j))],
        out_specs=pl.BlockSpec((tm, tq), lambda i, j: (i, j)),
        scratch_shapes=[h_scr],
        compiler_params=_params(2),
        name="in_proj_qkv",
    )(x, g_mix3, w_in, qkv_scale)

    tc = _tile(conv_ch, tn // 2)
    glu = pl.pallas_call(
        _glu_kernel,
        out_shape=jax.ShapeDtypeStruct((t, conv_ch), F32),
        grid=(t // tm, conv_ch // tc),
        in_specs=[x_spec, g_spec, w_spec(tc, qkv_w), w_spec(tc, qkv_w + conv_ch)],
        out_specs=pl.BlockSpec((tm, tc), lambda i, j: (i, j)),
        scratch_shapes=[h_scr],
        compiler_params=_params(2),
        name="in_proj_glu",
    )(x, g_mix3, w_in, w_in)

    tg = _tile(gate_w, tn)
    gates = pl.pallas_call(
        _gates_kernel,
        out_shape=jax.ShapeDtypeStruct((t, gate_w), BF16),
        grid=(t // tm, gate_w // tg),
        in_specs=[x_spec, g_spec, w_spec(tg, qkv_w + 2 * conv_ch),
                  pl.BlockSpec((None, 1, tg), lambda i, j: (l, 0, j))],
        out_specs=pl.BlockSpec((tm, tg), lambda i, j: (i, j)),
        scratch_shapes=[h_scr],
        compiler_params=_params(2),
        name="in_proj_gates",
    )(x, g_mix3, w_in, b_gate3)
    return qkv, glu, gates


TW = 256


def _attn_kernel(slopes_ref, lq_ref, lk_ref, gcol_ref, q_ref, k_ref, v_ref, o_ref,
                 tab_ref, vt_ref, qbd_ref, st_ref, pt_ref, acc_ref, mb_ref, *, lam_init, head_dim, rk):
    hd = head_dim
    vd = 2 * hd
    h = pl.program_id(0)
    b = pl.program_id(1)
    qi = pl.program_id(2)
    s_len = k_ref.shape[0]
    n_chunks = s_len // rk
    nt = 2 * TW

    @pl.when((b == 0) & (qi == 0))
    def _():
        neg = -(slopes_ref[h] * LOG2E)
        base = (lax.broadcasted_iota(jnp.int32, (rk, TW), 0)
                - lax.broadcasted_iota(jnp.int32, (rk, TW), 1) - (s_len - TW))
        n_rows = tab_ref.shape[0]
        for r0 in range(0, n_rows, rk):
            rw = min(rk, n_rows - r0)
            tab_ref[r0:r0 + rw, :] = neg * jnp.abs(base[:rw] + r0).astype(F32)

    @pl.when(qi == 0)
    def _():
        for r0 in range(0, s_len, rk):
            vt_ref[:, r0:r0 + rk] = v_ref[r0:r0 + rk, :].T

    lq = lq_ref[...]
    lk = lk_ref[...]
    lam = (jnp.exp(jnp.sum(lq[0:1] * lk[0:1], axis=-1, keepdims=True))
           - jnp.exp(jnp.sum(lq[1:2] * lk[1:2], axis=-1, keepdims=True)) + lam_init)

    zero = jnp.zeros((TW, hd), q_ref.dtype)
    for u in range(2):
        qbd_ref[u, 0:TW, 0:hd] = q_ref[u * TW:(u + 1) * TW, 0:hd]
        qbd_ref[u, 0:TW, hd:vd] = zero
        qbd_ref[u, TW:nt, 0:hd] = zero
        qbd_ref[u, TW:nt, hd:vd] = q_ref[u * TW:(u + 1) * TW, hd:vd]

    def fold(x):
        return x.reshape(rk // SUBLANES, SUBLANES, x.shape[-1])

    def scores(u, j, mx):
        rows = slice(j * rk, (j + 1) * rk)
        off = s_len - TW - (qi * 2 * TW + u * TW)
        sc = lax.dot_general(k_ref[rows, :], qbd_ref[u], (((1,), (1,)), ((), ())),
                             preferred_element_type=F32)
        tabv = tab_ref[pl.ds(pl.multiple_of(off + j * rk, TW), rk), :]
        t = jnp.concatenate([sc[:, :TW] + tabv, sc[:, TW:] + tabv], axis=1)
        st_ref[u, rows, :] = t
        return jnp.maximum(mx, jnp.max(fold(t), axis=0))

    def finish_max(u, mx):
        mb_ref[u] = jnp.broadcast_to(jnp.max(mx, axis=0, keepdims=True), (SUBLANES, nt))

    def exps(u, j, ls):
        rows = slice(j * rk, (j + 1) * rk)
        p = jnp.exp2(fold(st_ref[u, rows, :]) - mb_ref[u][None])
        pt_ref[u, rows, :] = p.reshape(rk, nt).astype(pt_ref.dtype)
        return ls + jnp.sum(p, axis=0)

    def pv(u, j):
        keys = slice(j * rk, (j + 1) * rk)
        acc_ref[u] += jnp.dot(vt_ref[:, keys], pt_ref[u, keys, :], preferred_element_type=F32)

    def finish(u, ls):
        l = jnp.sum(ls, axis=0, keepdims=True)
        ot2 = acc_ref[u]
        ot = ot2[:, :TW] * (1.0 / l[:, :TW]) - ot2[:, TW:] * (lam / l[:, TW:])
        yt = ot * lax.rsqrt(jnp.mean(ot * ot, axis=0, keepdims=True) + NORM_EPS)
        yt = (yt * gcol_ref[...]) * (1.0 - lam_init)
        o_ref[u * TW:(u + 1) * TW, :] = yt.T.astype(o_ref.dtype)

    def chunks(body, init):
        c = init
        for j in range(n_chunks):
            c = body(j, c)
        return c

    neg_inf = jnp.full((SUBLANES, nt), -jnp.inf, F32)
    zeros8 = jnp.zeros((SUBLANES, nt), F32)
    acc_ref[...] = jnp.zeros_like(acc_ref)

    mx0 = chunks(lambda j, mx: scores(0, j, mx), neg_inf)
    finish_max(0, mx0)
    mx1, ls0 = chunks(lambda j, c: (scores(1, j, c[0]), exps(0, j, c[1])), (neg_inf, zeros8))
    finish_max(1, mx1)

    def exp1_pv0(j, ls):
        pv(0, j)
        return exps(1, j, ls)

    ls1 = chunks(exp1_pv0, zeros8)

    def pv1(j, c):
        pv(1, j)
        return c

    chunks(pv1, 0)
    finish(0, ls0)
    finish(1, ls1)


def _attention(qkv, slopes, lambda_q, lambda_k, g_subln_col, l, lam_init, n_seq, s_len, n_heads, head_dim, rk):
    v_dim = 2 * head_dim
    tq = 2 * TW
    assert s_len % tq == 0 and v_dim == TW
    rk = _tile(s_len, rk)
    nq = s_len // tq
    kern = functools.partial(_attn_kernel, lam_init=lam_init, head_dim=head_dim, rk=rk)
    return pl.pallas_call(
        kern,
        out_shape=jax.ShapeDtypeStruct((n_seq * s_len, n_heads * v_dim), BF16),
        grid=(n_heads, n_seq, nq),
        in_specs=[
            pl.BlockSpec(memory_space=pltpu.SMEM),
            pl.BlockSpec((None, 2, head_dim), lambda h, b, i: (l, 0, 0)),
            pl.BlockSpec((None, 2, head_dim), lambda h, b, i: (l, 0, 0)),
            pl.BlockSpec((None, v_dim, 1), lambda h, b, i: (l, 0, 0)),
            pl.BlockSpec((tq, v_dim), lambda h, b, i: (b * nq + i, h)),
            pl.BlockSpec((s_len, v_dim), lambda h, b, i: (b, n_heads + h)),
            pl.BlockSpec((s_len, v_dim), lambda h, b, i: (b, 2 * n_heads + h)),
        ],
        out_specs=pl.BlockSpec((tq, v_dim), lambda h, b, i: (b * nq + i, h)),
        scratch_shapes=[
            pltpu.VMEM((2 * s_len - TW, TW), F32),
            pltpu.VMEM((v_dim, s_len), BF16),
            pltpu.VMEM((2, 2 * TW, v_dim), BF16),
            pltpu.VMEM((2, s_len, 2 * TW), F32),
            pltpu.VMEM((2, s_len, 2 * TW), BF16),
            pltpu.VMEM((2, v_dim, 2 * TW), F32),
            pltpu.VMEM((2, SUBLANES, 2 * TW), F32),
        ],
        compiler_params=pltpu.CompilerParams(dimension_semantics=("arbitrary",) * 3,
                                             vmem_limit_bytes=VMEM_LIMIT),
        name="diff_attention",
    )(slopes, lambda_q, lambda_k, g_subln_col, qkv, qkv, qkv)


def _conv_kernel(prev_ref, main_ref, next_ref, w_ref, bdw_ref, gln_ref, bln_ref, o_ref, win_ref, sh_ref, acc_ref,
                 *, tt, n_taps, tiles_per_seq, rows, lanes):
    pos = pl.program_id(0) % tiles_per_seq
    win_ref[0:HALO, :] = jnp.where(pos == 0, 0.0, prev_ref[...])
    win_ref[HALO:HALO + tt, :] = main_ref[...]
    win_ref[HALO + tt:, :] = jnp.where(pos == tiles_per_seq - 1, 0.0, next_ref[...])

    span = sh_ref.shape[1]
    for s in range(1, SUBLANES):
        sh_ref[s - 1] = win_ref[s:s + span, :]

    first = HALO - (n_taps - 1) // 2
    def lane_chunk(ci, carry):
        cols = pl.ds(pl.multiple_of(ci * lanes, lanes), lanes)
        for r0 in range(0, tt, rows):
            acc = None
            for s in range(SUBLANES):
                offs = [d for d in range(first, first + n_taps) if d % SUBLANES == s]
                if not offs:
                    continue
                lo, hi = offs[0] - s, offs[-1] - s
                span_rows = slice(r0 + lo, r0 + hi + rows)
                slab = win_ref[span_rows, cols] if s == 0 else sh_ref[s - 1, span_rows, cols]
                for d in offs:
                    k = d - first
                    term = w_ref[k:k + 1, cols] * slab[d - s - lo:d - s - lo + rows]
                    acc = term if acc is None else acc + term
            acc_ref[r0:r0 + rows, cols] = acc + bdw_ref[:, cols]
        return carry

    lax.fori_loop(0, main_ref.shape[1] // lanes, lane_chunk, 0)

    y = acc_ref[...]
    yc = y - jnp.mean(y, axis=-1, keepdims=True)
    yn = yc * lax.rsqrt(jnp.mean(yc * yc, axis=-1, keepdims=True) + LN_EPS)
    z = yn * gln_ref[...] + bln_ref[...]
    o_ref[...] = (z * jax.nn.sigmoid(z)).astype(o_ref.dtype)


def _conv_module(glu, w_dw, b_dw3, g_ln3, b_ln3, l, s_len, tt):
    t, c = glu.shape
    n_taps = w_dw.shape[1]
    pad = (n_taps - 1) // 2
    assert pad <= HALO
    tt = _tile(s_len, tt)
    assert tt % HALO == 0 and t % s_len == 0
    hb = tt // HALO
    n_hblk = t // HALO
    span = tt + (HALO + pad) // SUBLANES * SUBLANES
    vec = pl.BlockSpec((None, 1, c), lambda i: (l, 0, 0))
    kern = functools.partial(_conv_kernel, tt=tt, n_taps=n_taps, tiles_per_seq=s_len // tt, rows=64, lanes=LANES)
    return pl.pallas_call(
        kern,
        out_shape=jax.ShapeDtypeStruct((t, c), BF16),
        grid=(t // tt,),
        in_specs=[
            pl.BlockSpec((HALO, c), lambda i: (jnp.maximum(i * hb - 1, 0), 0)),
            pl.BlockSpec((tt, c), lambda i: (i, 0)),
            pl.BlockSpec((HALO, c), lambda i: (jnp.minimum((i + 1) * hb, n_hblk - 1), 0)),
            pl.BlockSpec((None, n_taps, c), lambda i: (l, 0, 0)),
            vec, vec, vec,
        ],
        out_specs=pl.BlockSpec((tt, c), lambda i: (i, 0)),
        scratch_shapes=[pltpu.VMEM((tt + 2 * HALO, c), F32),
                        pltpu.VMEM((SUBLANES - 1, span, c), F32),
                        pltpu.VMEM((tt, c), F32)],
        compiler_params=_params(1),
        name="conv_module",
    )(glu, glu, glu, w_dw, b_dw3, g_ln3, b_ln3)


def _merge_kernel(a_ref, c_ref, wa_ref, wc_ref, bc_ref, ga_ref, gc_ref, o_ref):
    a = jnp.dot(a_ref[...], wa_ref[...], preferred_element_type=F32)
    c = jnp.dot(c_ref[...], wc_ref[...], preferred_element_type=F32) + bc_ref[...]
    o_ref[...] = (ga_ref[...] * a + gc_ref[...] * c).astype(o_ref.dtype)


def _merge(attn, conv, w_attn_proj, w_conv_proj, b_conv_proj3, gates, l, tm, tn):
    t, kv = attn.shape
    kc = conv.shape[1]
    d = w_attn_proj.shape[2]
    tm = _tile(t, tm)
    tn = _tile(d, tn)
    nj = d // tn
    return pl.pallas_call(
        _merge_kernel,
        out_shape=jax.ShapeDtypeStruct((t, d), BF16),
        grid=(t // tm, nj),
        in_specs=[
            pl.BlockSpec((tm, kv), lambda i, j: (i, 0)),
            pl.BlockSpec((tm, kc), lambda i, j: (i, 0)),
            pl.BlockSpec((None, kv, tn), lambda i, j: (l, 0, j)),
            pl.BlockSpec((None, kc, tn), lambda i, j: (l, 0, j)),
            pl.BlockSpec((None, 1, tn), lambda i, j: (l, 0, j)),
            pl.BlockSpec((tm, tn), lambda i, j: (i, j)),
            pl.BlockSpec((tm, tn), lambda i, j: (i, nj + j)),
        ],
        out_specs=pl.BlockSpec((tm, tn), lambda i, j: (i, j)),
        compiler_params=_params(2),
        name="branch_merge",
    )(attn, conv, w_attn_proj, w_conv_proj, b_conv_proj3, gates, gates)


def _residual_matmul_kernel(a_ref, w_ref, x_ref, o_ref):
    o_ref[...] = x_ref[...] + jnp.dot(a_ref[...], w_ref[...], preferred_element_type=F32)


def _residual_matmul(a, w, x, l, tm, tn, name):
    t, k = a.shape
    d = w.shape[2]
    tm = _tile(t, tm)
    tn = _tile(d, tn)
    return pl.pallas_call(
        _residual_matmul_kernel,
        out_shape=jax.ShapeDtypeStruct((t, d), F32),
        grid=(t // tm, d // tn),
        in_specs=[
            pl.BlockSpec((tm, k), lambda i, j: (i, 0)),
            pl.BlockSpec((None, k, tn), lambda i, j: (l, 0, j)),
            pl.BlockSpec((tm, tn), lambda i, j: (i, j)),
        ],
        out_specs=pl.BlockSpec((tm, tn), lambda i, j: (i, j)),
        compiler_params=_params(2),
        name=name,
    )(a, w, x)


def _ffn_in_kernel(x_ref, g_ref, wg_ref, wu_ref, o_ref, h_ref):
    @pl.when(pl.program_id(1) == 0)
    def _():
        _rmsnorm_to(x_ref, g_ref, h_ref)

    h = h_ref[...]
    gate = jnp.dot(h, wg_ref[...], preferred_element_type=F32)
    up = jnp.dot(h, wu_ref[...], preferred_element_type=F32)
    o_ref[...] = ((gate * jax.nn.sigmoid(gate)) * up).astype(o_ref.dtype)


def _ffn_in(x, g_ffn3, w_ffn_in, l, tm, tn):
    t, d = x.shape
    d_ff = w_ffn_in.shape[2] // 2
    tm = _tile(t, tm)
    tn = _tile(d_ff, tn)
    nj = d_ff // tn
    return pl.pallas_call(
        _ffn_in_kernel,
        out_shape=jax.ShapeDtypeStruct((t, d_ff), BF16),
        grid=(t // tm, nj),
        in_specs=[
            pl.BlockSpec((tm, d), lambda i, j: (i, 0)),
            pl.BlockSpec((None, 1, d), lambda i, j: (l, 0, 0)),
            pl.BlockSpec((None, d, tn), lambda i, j: (l, 0, j)),
            pl.BlockSpec((None, d, tn), lambda i, j: (l, 0, nj + j)),
        ],
        out_specs=pl.BlockSpec((tm, tn), lambda i, j: (i, j)),
        scratch_shapes=[pltpu.VMEM((tm, d), BF16)],
        compiler_params=_params(2),
        name="ffn_in",
    )(x, g_ffn3, w_ffn_in, w_ffn_in)


def _final_norm_kernel(x_ref, g_ref, o_ref):
    _rmsnorm_to(x_ref, g_ref, o_ref)


def _final_norm(x, g2, tm):
    t, d = x.shape
    tm = _tile(t, tm)
    return pl.pallas_call(
        _final_norm_kernel,
        out_shape=jax.ShapeDtypeStruct((t, d), F32),
        grid=(t // tm,),
        in_specs=[pl.BlockSpec((tm, d), lambda i: (i, 0)),
                  pl.BlockSpec((1, d), lambda i: (0, 0))],
        out_specs=pl.BlockSpec((tm, d), lambda i: (i, 0)),
        compiler_params=_params(1),
        name="final_norm",
    )(x, g2)


def _lambda_init(layer):
    return 0.8 - 0.6 * float(np.exp(-0.3 * layer))


def kernel(x_prompt, x_sample, g_mix, w_in, b_gate, lambda_q, lambda_k, g_subln, w_attn_proj, w_dw, b_dw,
           g_conv_ln, b_conv_ln, w_conv_proj, b_conv_proj, w_out, g_ffn, w_ffn_in, w_ffn_out, g_final):
    d = x_prompt.shape[2]
    depth = w_in.shape[0]
    head_dim = lambda_q.shape[2]
    v_width = w_attn_proj.shape[1]
    n_heads = v_width // (2 * head_dim)
    qk_width = n_heads * 2 * head_dim
    conv_ch = w_conv_proj.shape[1]
    gate_w = b_gate.shape[1]
    qkv_w = 2 * qk_width + v_width
    assert w_in.shape[2] == qkv_w + 2 * conv_ch + gate_w and gate_w == 2 * d

    slopes = jnp.asarray(2.0 ** (-8.0 * np.arange(1, n_heads + 1) / n_heads), dtype=F32)
    qkv_scale = jnp.asarray(np.concatenate([np.full(qk_width, head_dim ** -0.5 * LOG2E),
                                            np.ones(qkv_w - qk_width)])[None, :], dtype=F32)

    def row(p):
        return p.reshape(p.shape[0], 1, p.shape[1])

    w_in_b = w_in.astype(BF16)
    w_attn_b = w_attn_proj.astype(BF16)
    w_conv_b = w_conv_proj.astype(BF16)
    w_out_b = w_out.astype(BF16)
    w_ffn_in_b = w_ffn_in.astype(BF16)
    w_ffn_out_b = w_ffn_out.astype(BF16)
    w_dw3 = w_dw.reshape(depth, w_dw.shape[1], conv_ch)
    g_mix3, b_gate3, b_dw3 = row(g_mix), row(b_gate), row(b_dw)
    g_subln_col = g_subln.reshape(depth, g_subln.shape[1], 1)
    g_ln3, b_ln3, b_pw3, g_ffn3 = row(g_conv_ln), row(b_conv_ln), row(b_conv_proj), row(g_ffn)
    g2 = g_final.reshape(1, d)

    def trunk(x3):
        n_seq, s_len, _ = x3.shape
        x = x3.reshape(n_seq * s_len, d)
        for l in range(depth):
            lam_init = _lambda_init(l)
            qkv, glu, gates = _in_proj(x, g_mix3, w_in_b, b_gate3, qkv_scale, l, (qkv_w, conv_ch, gate_w),
                                       tm=1024, tn=1024)
            attn = _attention(qkv, slopes, lambda_q, lambda_k, g_subln_col, l, lam_init, n_seq, s_len, n_heads,
                              head_dim, rk=512)
            conv = _conv_module(glu, w_dw3, b_dw3, g_ln3, b_ln3, l, s_len, tt=128)
            merged = _merge(attn, conv, w_attn_b, w_conv_b, b_pw3, gates, l, tm=1024, tn=512)
            x = _residual_matmul(merged, w_out_b, x, l, tm=1024, tn=1024, name="out_proj")
            act = _ffn_in(x, g_ffn3, w_ffn_in_b, l, tm=1024, tn=512)
            x = _residual_matmul(act, w_ffn_out_b, x, l, tm=512, tn=512, name="ffn_out")
        return _final_norm(x, g2, tm=512).reshape(x3.shape)

    return (trunk(x_prompt), trunk(x_sample))
```

```python
import functools

import numpy as np
import jax
import jax.numpy as jnp
from jax import lax
from jax.experimental import pallas as pl
from jax.experimental.pallas import tpu as pltpu

NORM_EPS = 1e-6
LN_EPS = 1e-5
LOG2E = 1.4426950408889634
SUBLANES = 8
LANES = 128
HALO = 16
VMEM_LIMIT = 52 * 1024 * 1024

F32 = jnp.float32
BF16 = jnp.bfloat16


def _params(n_axes):
    sem = ("parallel",) + ("arbitrary",) * (n_axes - 1)
    return pltpu.CompilerParams(dimension_semantics=sem, vmem_limit_bytes=VMEM_LIMIT)


def _tile(n, want):
    t = min(n, want)
    assert n % t == 0, (n, want)
    return t


def _rmsnorm_to(x_ref, g_ref, h_ref):
    xf = x_ref[...]
    y = xf * lax.rsqrt(jnp.mean(xf * xf, axis=-1, keepdims=True) + NORM_EPS)
    h_ref[...] = (y * g_ref[...]).astype(h_ref.dtype)


def _qkv_kernel(x_ref, g_ref, w_ref, cs_ref, o_ref, h_ref):
    @pl.when(pl.program_id(1) == 0)
    def _():
        _rmsnorm_to(x_ref, g_ref, h_ref)

    z = jnp.dot(h_ref[...], w_ref[...], preferred_element_type=F32)
    o_ref[...] = (z * cs_ref[...]).astype(o_ref.dtype)


def _glu_kernel(x_ref, g_ref, wv_ref, wg_ref, o_ref, h_ref):
    @pl.when(pl.program_id(1) == 0)
    def _():
        _rmsnorm_to(x_ref, g_ref, h_ref)

    h = h_ref[...]
    val = jnp.dot(h, wv_ref[...], preferred_element_type=F32)
    gate = jnp.dot(h, wg_ref[...], preferred_element_type=F32)
    o_ref[...] = (val * jax.nn.sigmoid(gate)).astype(o_ref.dtype)


def _gates_kernel(x_ref, g_ref, w_ref, b_ref, o_ref, h_ref):
    @pl.when(pl.program_id(1) == 0)
    def _():
        _rmsnorm_to(x_ref, g_ref, h_ref)

    z = jnp.dot(h_ref[...], w_ref[...], preferred_element_type=F32)
    o_ref[...] = jax.nn.sigmoid(z + b_ref[...]).astype(o_ref.dtype)


def _in_proj(x, g_mix3, w_in, b_gate3, qkv_scale, l, widths, tm, tn):
    t, d = x.shape
    qkv_w, conv_ch, gate_w = widths
    tm = _tile(t, tm)
    x_spec = pl.BlockSpec((tm, d), lambda i, j: (i, 0))
    g_spec = pl.BlockSpec((None, 1, d), lambda i, j: (l, 0, 0))
    h_scr = pltpu.VMEM((tm, d), BF16)

    def w_spec(width, col0):
        assert col0 % width == 0
        return pl.BlockSpec((None, d, width), lambda i, j: (l, 0, col0 // width + j))

    tq = _tile(qkv_w, tn)
    qkv = pl.pallas_call(
        _qkv_kernel,
        out_shape=jax.ShapeDtypeStruct((t, qkv_w), BF16),
        grid=(t // tm, qkv_w // tq),
        in_specs=[x_spec, g_spec, w_spec(tq, 0), pl.BlockSpec((1, tq), lambda i, j: (0, j))],
        out_specs=pl.BlockSpec((tm, tq), lambda i, j: (i, j)),
        scratch_shapes=[h_scr],
        compiler_params=_params(2),
        name="in_proj_qkv",
    )(x, g_mix3, w_in, qkv_scale)

    tc = _tile(conv_ch, tn // 2)
    glu = pl.pallas_call(
        _glu_kernel,
        out_shape=jax.ShapeDtypeStruct((t, conv_ch), F32),
        grid=(t // tm, conv_ch // tc),
        in_specs=[x_spec, g_spec, w_spec(tc, qkv_w), w_spec(tc, qkv_w + conv_ch)],
        out_specs=pl.BlockSpec((tm, tc), lambda i, j: (i, j)),
        scratch_shapes=[h_scr],
        compiler_params=_params(2),
        name="in_proj_glu",
    )(x, g_mix3, w_in, w_in)

    tg = _tile(gate_w, tn)
    gates = pl.pallas_call(
        _gates_kernel,
        out_shape=jax.ShapeDtypeStruct((t, gate_w), BF16),
        grid=(t // tm, gate_w // tg),
        in_specs=[x_spec, g_spec, w_spec(tg, qkv_w + 2 * conv_ch),
                  pl.BlockSpec((None, 1, tg), lambda i, j: (l, 0, j))],
        out_specs=pl.BlockSpec((tm, tg), lambda i, j: (i, j)),
        scratch_shapes=[h_scr],
        compiler_params=_params(2),
        name="in_proj_gates",
    )(x, g_mix3, w_in, b_gate3)
    return qkv, glu, gates


TW = 256


def _attn_kernel(slopes_ref, lq_ref, lk_ref, gcol_ref, q_ref, k_ref, v_ref, o_ref,
                 tab_ref, vt_ref, qbd_ref, st_ref, pt_ref, acc_ref, mb_ref, *, lam_init, head_dim, rk):
    hd = head_dim
    vd = 2 * hd
    h = pl.program_id(0)
    b = pl.program_id(1)
    qi = pl.program_id(2)
    s_len = k_ref.shape[0]
    n_chunks = s_len // rk
    nt = 2 * TW

    @pl.when((b == 0) & (qi == 0))
    def _():
        neg = -(slopes_ref[h] * LOG2E)
        base = (lax.broadcasted_iota(jnp.int32, (rk, TW), 0)
                - lax.broadcasted_iota(jnp.int32, (rk, TW), 1) - (s_len - TW))
        n_rows = tab_ref.shape[0]
        for r0 in range(0, n_rows, rk):
            rw = min(rk, n_rows - r0)
            tab_ref[r0:r0 + rw, :] = neg * jnp.abs(base[:rw] + r0).astype(F32)

    @pl.when(qi == 0)
    def _():
        for r0 in range(0, s_len, rk):
            vt_ref[:, r0:r0 + rk] = v_ref[r0:r0 + rk, :].T

    lq = lq_ref[...]
    lk = lk_ref[...]
    lam = (jnp.exp(jnp.sum(lq[0:1] * lk[0:1], axis=-1, keepdims=True))
           - jnp.exp(jnp.sum(lq[1:2] * lk[1:2], axis=-1, keepdims=True)) + lam_init)

    zero = jnp.zeros((TW, hd), q_ref.dtype)
    for u in range(2):
        qbd_ref[u, 0:TW, 0:hd] = q_ref[u * TW:(u + 1) * TW, 0:hd]
        qbd_ref[u, 0:TW, hd:vd] = zero
        qbd_ref[u, TW:nt, 0:hd] = zero
        qbd_ref[u, TW:nt, hd:vd] = q_ref[u * TW:(u + 1) * TW, hd:vd]

    def fold(x):
        return x.reshape(rk // SUBLANES, SUBLANES, x.shape[-1])

    def scores(u, j, mx):
        rows = slice(j * rk, (j + 1) * rk)
        off = s_len - TW - (qi * 2 * TW + u * TW)
        sc = lax.dot_general(k_ref[rows, :], qbd_ref[u], (((1,), (1,)), ((), ())),
                             preferred_element_type=F32)
        tabv = tab_ref[pl.ds(pl.multiple_of(off + j * rk, TW), rk), :]
        t = jnp.concatenate([sc[:, :TW] + tabv, sc[:, TW:] + tabv], axis=1)
        st_ref[u, rows, :] = t
        return jnp.maximum(mx, jnp.max(fold(t), axis=0))

    def finish_max(u, mx):
        mb_ref[u] = jnp.broadcast_to(jnp.max(mx, axis=0, keepdims=True), (SUBLANES, nt))

    def exps(u, j, ls):
        rows = slice(j * rk, (j + 1) * rk)
        p = jnp.exp2(fold(st_ref[u, rows, :]) - mb_ref[u][None])
        pt_ref[u, rows, :] = p.reshape(rk, nt).astype(pt_ref.dtype)
        return ls + jnp.sum(p, axis=0)

    def pv(u, j):
        keys = slice(j * rk, (j + 1) * rk)
        acc_ref[u] += jnp.dot(vt_ref[:, keys], pt_ref[u, keys, :], preferred_element_type=F32)

    def finish(u, ls):
        l = jnp.sum(ls, axis=0, keepdims=True)
        ot2 = acc_ref[u]
        ot = ot2[:, :TW] * (1.0 / l[:, :TW]) - ot2[:, TW:] * (lam / l[:, TW:])
        yt = ot * lax.rsqrt(jnp.mean(ot * ot, axis=0, keepdims=True) + NORM_EPS)
        yt = (yt * gcol_ref[...]) * (1.0 - lam_init)
        o_ref[u * TW:(u + 1) * TW, :] = yt.T.astype(o_ref.dtype)

    def chunks(body, init):
        c = init
        for j in range(n_chunks):
            c = body(j, c)
        return c

    neg_inf = jnp.full((SUBLANES, nt), -jnp.inf, F32)
    zeros8 = jnp.zeros((SUBLANES, nt), F32)
    acc_ref[...] = jnp.zeros_like(acc_ref)

    mx0 = chunks(lambda j, mx: scores(0, j, mx), neg_inf)
    finish_max(0, mx0)
    mx1, ls0 = chunks(lambda j, c: (scores(1, j, c[0]), exps(0, j, c[1])), (neg_inf, zeros8))
    finish_max(1, mx1)

    def exp1_pv0(j, ls):
        pv(0, j)
        return exps(1, j, ls)

    ls1 = chunks(exp1_pv0, zeros8)

    def pv1(j, c):
        pv(1, j)
        return c

    chunks(pv1, 0)
    finish(0, ls0)
    finish(1, ls1)


def _attention(qkv, slopes, lambda_q, lambda_k, g_subln_col, l, lam_init, n_seq, s_len, n_heads, head_dim, rk):
    v_dim = 2 * head_dim
    tq = 2 * TW
    assert s_len % tq == 0 and v_dim == TW
    rk = _tile(s_len, rk)
    nq = s_len // tq
    kern = functools.partial(_attn_kernel, lam_init=lam_init, head_dim=head_dim, rk=rk)
    return pl.pallas_call(
        kern,
        out_shape=jax.ShapeDtypeStruct((n_seq * s_len, n_heads * v_dim), BF16),
        grid=(n_heads, n_seq, nq),
        in_specs=[
            pl.BlockSpec(memory_space=pltpu.SMEM),
            pl.BlockSpec((None, 2, head_dim), lambda h, b, i: (l, 0, 0)),
            pl.BlockSpec((None, 2, head_dim), lambda h, b, i: (l, 0, 0)),
            pl.BlockSpec((None, v_dim, 1), lambda h, b, i: (l, 0, 0)),
            pl.BlockSpec((tq, v_dim), lambda h, b, i: (b * nq + i, h)),
            pl.BlockSpec((s_len, v_dim), lambda h, b, i: (b, n_heads + h)),
            pl.BlockSpec((s_len, v_dim), lambda h, b, i: (b, 2 * n_heads + h)),
        ],
        out_specs=pl.BlockSpec((tq, v_dim), lambda h, b, i: (b * nq + i, h)),
        scratch_shapes=[
            pltpu.VMEM((2 * s_len - TW, TW), F32),
            pltpu.VMEM((v_dim, s_len), BF16),
            pltpu.VMEM((2, 2 * TW, v_dim), BF16),
            pltpu.VMEM((2, s_len, 2 * TW), F32),
            pltpu.VMEM((2, s_len, 2 * TW), BF16),
            pltpu.VMEM((2, v_dim, 2 * TW), F32),
            pltpu.VMEM((2, SUBLANES, 2 * TW), F32),
        ],
        compiler_params=pltpu.CompilerParams(dimension_semantics=("arbitrary",) * 3,
                                             vmem_limit_bytes=VMEM_LIMIT),
        name="diff_attention",
    )(slopes, lambda_q, lambda_k, g_subln_col, qkv, qkv, qkv)


def _conv_kernel(prev_ref, main_ref, next_ref, w_ref, bdw_ref, gln_ref, bln_ref, o_ref, win_ref, sh_ref, acc_ref,
                 *, tt, n_taps, tiles_per_seq, rows, lanes):
    pos = pl.program_id(0) % tiles_per_seq
    win_ref[0:HALO, :] = jnp.where(pos == 0, 0.0, prev_ref[...])
    win_ref[HALO:HALO + tt, :] = main_ref[...]
    win_ref[HALO + tt:, :] = jnp.where(pos == tiles_per_seq - 1, 0.0, next_ref[...])

    span = sh_ref.shape[1]
    for s in range(1, SUBLANES):
        sh_ref[s - 1] = win_ref[s:s + span, :]

    first = HALO - (n_taps - 1) // 2
    def lane_chunk(ci, carry):
        cols = pl.ds(pl.multiple_of(ci * lanes, lanes), lanes)
        for r0 in range(0, tt, rows):
            acc = None
            for s in range(SUBLANES):
                offs = [d for d in range(first, first + n_taps) if d % SUBLANES == s]
                if not offs:
                    continue
                lo, hi = offs[0] - s, offs[-1] - s
                span_rows = slice(r0 + lo, r0 + hi + rows)
                slab = win_ref[span_rows, cols] if s == 0 else sh_ref[s - 1, span_rows, cols]
                for d in offs:
                    k = d - first
                    term = w_ref[k:k + 1, cols] * slab[d - s - lo:d - s - lo + rows]
                    acc = term if acc is None else acc + term
            acc_ref[r0:r0 + rows, cols] = acc + bdw_ref[:, cols]
        return carry

    lax.fori_loop(0, main_ref.shape[1] // lanes, lane_chunk, 0)

    y = acc_ref[...]
    yc = y - jnp.mean(y, axis=-1, keepdims=True)
    yn = yc * lax.rsqrt(jnp.mean(yc * yc, axis=-1, keepdims=True) + LN_EPS)
    z = yn * gln_ref[...] + bln_ref[...]
    o_ref[...] = (z * jax.nn.sigmoid(z)).astype(o_ref.dtype)


def _conv_module(glu, w_dw, b_dw3, g_ln3, b_ln3, l, s_len, tt):
    t, c = glu.shape
    n_taps = w_dw.shape[1]
    pad = (n_taps - 1) // 2
    assert pad <= HALO
    tt = _tile(s_len, tt)
    assert tt % HALO == 0 and t % s_len == 0
    hb = tt // HALO
    n_hblk = t // HALO
    span = tt + (HALO + pad) // SUBLANES * SUBLANES
    vec = pl.BlockSpec((None, 1, c), lambda i: (l, 0, 0))
    kern = functools.partial(_conv_kernel, tt=tt, n_taps=n_taps, tiles_per_seq=s_len // tt, rows=64, lanes=LANES)
    return pl.pallas_call(
        kern,
        out_shape=jax.ShapeDtypeStruct((t, c), BF16),
        grid=(t // tt,),
        in_specs=[
            pl.BlockSpec((HALO, c), lambda i: (jnp.maximum(i * hb - 1, 0), 0)),
            pl.BlockSpec((tt, c), lambda i: (i, 0)),
            pl.BlockSpec((HALO, c), lambda i: (jnp.minimum((i + 1) * hb, n_hblk - 1), 0)),
            pl.BlockSpec((None, n_taps, c), lambda i: (l, 0, 0)),
            vec, vec, vec,
        ],
        out_specs=pl.BlockSpec((tt, c), lambda i: (i, 0)),
        scratch_shapes=[pltpu.VMEM((tt + 2 * HALO, c), F32),
                        pltpu.VMEM((SUBLANES - 1, span, c), F32),
                        pltpu.VMEM((tt, c), F32)],
        compiler_params=_params(1),
        name="conv_module",
    )(glu, glu, glu, w_dw, b_dw3, g_ln3, b_ln3)


def _merge_kernel(a_ref, c_ref, wa_ref, wc_ref, bc_ref, ga_ref, gc_ref, o_ref):
    a = jnp.dot(a_ref[...], wa_ref[...], preferred_element_type=F32)
    c = jnp.dot(c_ref[...], wc_ref[...], preferred_element_type=F32) + bc_ref[...]
    o_ref[...] = (ga_ref[...] * a + gc_ref[...] * c).astype(o_ref.dtype)


def _merge(attn, conv, w_attn_proj, w_conv_proj, b_conv_proj3, gates, l, tm, tn):
    t, kv = attn.shape
    kc = conv.shape[1]
    d = w_attn_proj.shape[2]
    tm = _tile(t, tm)
    tn = _tile(d, tn)
    nj = d // tn
    return pl.pallas_call(
        _merge_kernel,
        out_shape=jax.ShapeDtypeStruct((t, d), BF16),
        grid=(t // tm, nj),
        in_specs=[
            pl.BlockSpec((tm, kv), lambda i, j: (i, 0)),
            pl.BlockSpec((tm, kc), lambda i, j: (i, 0)),
            pl.BlockSpec((None, kv, tn), lambda i, j: (l, 0, j)),
            pl.BlockSpec((None, kc, tn), lambda i, j: (l, 0, j)),
            pl.BlockSpec((None, 1, tn), lambda i, j: (l, 0, j)),
            pl.BlockSpec((tm, tn), lambda i, j: (i, j)),
            pl.BlockSpec((tm, tn), lambda i, j: (i, nj + j)),
        ],
        out_specs=pl.BlockSpec((tm, tn), lambda i, j: (i, j)),
        compiler_params=_params(2),
        name="branch_merge",
    )(attn, conv, w_attn_proj, w_conv_proj, b_conv_proj3, gates, gates)


def _residual_matmul_kernel(a_ref, w_ref, x_ref, o_ref):
    o_ref[...] = x_ref[...] + jnp.dot(a_ref[...], w_ref[...], preferred_element_type=F32)


def _residual_matmul(a, w, x, l, tm, tn, name):
    t, k = a.shape
    d = w.shape[2]
    tm = _tile(t, tm)
    tn = _tile(d, tn)
    return pl.pallas_call(
        _residual_matmul_kernel,
        out_shape=jax.ShapeDtypeStruct((t, d), F32),
        grid=(t // tm, d // tn),
        in_specs=[
            pl.BlockSpec((tm, k), lambda i, j: (i, 0)),
            pl.BlockSpec((None, k, tn), lambda i, j: (l, 0, j)),
            pl.BlockSpec((tm, tn), lambda i, j: (i, j)),
        ],
        out_specs=pl.BlockSpec((tm, tn), lambda i, j: (i, j)),
        compiler_params=_params(2),
        name=name,
    )(a, w, x)


def _ffn_in_kernel(x_ref, g_ref, wg_ref, wu_ref, o_ref, h_ref):
    @pl.when(pl.program_id(1) == 0)
    def _():
        _rmsnorm_to(x_ref, g_ref, h_ref)

    h = h_ref[...]
    gate = jnp.dot(h, wg_ref[...], preferred_element_type=F32)
    up = jnp.dot(h, wu_ref[...], preferred_element_type=F32)
    o_ref[...] = ((gate * jax.nn.sigmoid(gate)) * up).astype(o_ref.dtype)


def _ffn_in(x, g_ffn3, w_ffn_in, l, tm, tn):
    t, d = x.shape
    d_ff = w_ffn_in.shape[2] // 2
    tm = _tile(t, tm)
    tn = _tile(d_ff, tn)
    nj = d_ff // tn
    return pl.pallas_call(
        _ffn_in_kernel,
        out_shape=jax.ShapeDtypeStruct((t, d_ff), BF16),
        grid=(t // tm, nj),
        in_specs=[
            pl.BlockSpec((tm, d), lambda i, j: (i, 0)),
            pl.BlockSpec((None, 1, d), lambda i, j: (l, 0, 0)),
            pl.BlockSpec((None, d, tn), lambda i, j: (l, 0, j)),
            pl.BlockSpec((None, d, tn), lambda i, j: (l, 0, nj + j)),
        ],
        out_specs=pl.BlockSpec((tm, tn), lambda i, j: (i, j)),
        scratch_shapes=[pltpu.VMEM((tm, d), BF16)],
        compiler_params=_params(2),
        name="ffn_in",
    )(x, g_ffn3, w_ffn_in, w_ffn_in)


def _final_norm_kernel(x_ref, g_ref, o_ref):
    _rmsnorm_to(x_ref, g_ref, o_ref)


def _final_norm(x, g2, tm):
    t, d = x.shape
    tm = _tile(t, tm)
    return pl.pallas_call(
        _final_norm_kernel,
        out_shape=jax.ShapeDtypeStruct((t, d), F32),
        grid=(t // tm,),
        in_specs=[pl.BlockSpec((tm, d), lambda i: (i, 0)),
                  pl.BlockSpec((1, d), lambda i: (0, 0))],
        out_specs=pl.BlockSpec((tm, d), lambda i: (i, 0)),
        compiler_params=_params(1),
        name="final_norm",
    )(x, g2)


def _lambda_init(layer):
    return 0.8 - 0.6 * float(np.exp(-0.3 * layer))


def kernel(x_prompt, x_sample, g_mix, w_in, b_gate, lambda_q, lambda_k, g_subln, w_attn_proj, w_dw, b_dw,
           g_conv_ln, b_conv_ln, w_conv_proj, b_conv_proj, w_out, g_ffn, w_ffn_in, w_ffn_out, g_final):
    d = x_prompt.shape[2]
    depth = w_in.shape[0]
    head_dim = lambda_q.shape[2]
    v_width = w_attn_proj.shape[1]
    n_heads = v_width // (2 * head_dim)
    qk_width = n_heads * 2 * head_dim
    conv_ch = w_conv_proj.shape[1]
    gate_w = b_gate.shape[1]
    qkv_w = 2 * qk_width + v_width
    assert w_in.shape[2] == qkv_w + 2 * conv_ch + gate_w and gate_w == 2 * d

    slopes = jnp.asarray(2.0 ** (-8.0 * np.arange(1, n_heads + 1) / n_heads), dtype=F32)
    qkv_scale = jnp.asarray(np.concatenate([np.full(qk_width, head_dim ** -0.5 * LOG2E),
                                            np.ones(qkv_w - qk_width)])[None, :], dtype=F32)

    def row(p):
        return p.reshape(p.shape[0], 1, p.shape[1])

    w_in_b = w_in.astype(BF16)
    w_attn_b = w_attn_proj.astype(BF16)
    w_conv_b = w_conv_proj.astype(BF16)
    w_out_b = w_out.astype(BF16)
    w_ffn_in_b = w_ffn_in.astype(BF16)
    w_ffn_out_b = w_ffn_out.astype(BF16)
    w_dw3 = w_dw.reshape(depth, w_dw.shape[1], conv_ch)
    g_mix3, b_gate3, b_dw3 = row(g_mix), row(b_gate), row(b_dw)
    g_subln_col = g_subln.reshape(depth, g_subln.shape[1], 1)
    g_ln3, b_ln3, b_pw3, g_ffn3 = row(g_conv_ln), row(b_conv_ln), row(b_conv_proj), row(g_ffn)
    g2 = g_final.reshape(1, d)

    def trunk(x3):
        n_seq, s_len, _ = x3.shape
        x = x3.reshape(n_seq * s_len, d)
        for l in range(depth):
            lam_init = _lambda_init(l)
            qkv, glu, gates = _in_proj(x, g_mix3, w_in_b, b_gate3, qkv_scale, l, (qkv_w, conv_ch, gate_w),
                                       tm=1024, tn=1024)
            attn = _attention(qkv, slopes, lambda_q, lambda_k, g_subln_col, l, lam_init, n_seq, s_len, n_heads,
                              head_dim, rk=512)
            conv = _conv_module(glu, w_dw3, b_dw3, g_ln3, b_ln3, l, s_len, tt=256)
            merged = _merge(attn, conv, w_attn_b, w_conv_b, b_pw3, gates, l, tm=1024, tn=512)
            x = _residual_matmul(merged, w_out_b, x, l, tm=512, tn=2048, name="out_proj")
            act = _ffn_in(x, g_ffn3, w_ffn_in_b, l, tm=1024, tn=512)
            x = _residual_matmul(act, w_ffn_out_b, x, l, tm=1024, tn=512, name="ffn_out")
        return _final_norm(x, g2, tm=512).reshape(x3.shape)

    return (trunk(x_prompt), trunk(x_sample))
```

```python
import functools

import numpy as np
import jax
import jax.numpy as jnp
from jax import lax
from jax.experimental import pallas as pl
from jax.experimental.pallas import tpu as pltpu

NORM_EPS = 1e-6
LN_EPS = 1e-5
LOG2E = 1.4426950408889634
SUBLANES = 8
LANES = 128
HALO = 16
VMEM_LIMIT = 52 * 1024 * 1024

F32 = jnp.float32
BF16 = jnp.bfloat16


def _params(n_axes):
    sem = ("parallel",) + ("arbitrary",) * (n_axes - 1)
    return pltpu.CompilerParams(dimension_semantics=sem, vmem_limit_bytes=VMEM_LIMIT)


def _tile(n, want):
    t = min(n, want)
    assert n % t == 0, (n, want)
    return t


def _rmsnorm_to(x_ref, g_ref, h_ref):
    xf = x_ref[...]
    y = xf * lax.rsqrt(jnp.mean(xf * xf, axis=-1, keepdims=True) + NORM_EPS)
    h_ref[...] = (y * g_ref[...]).astype(h_ref.dtype)


def _qkv_kernel(x_ref, g_ref, w_ref, cs_ref, o_ref, h_ref):
    @pl.when(pl.program_id(1) == 0)
    def _():
        _rmsnorm_to(x_ref, g_ref, h_ref)

    z = jnp.dot(h_ref[...], w_ref[...], preferred_element_type=F32)
    o_ref[...] = (z * cs_ref[...]).astype(o_ref.dtype)


def _glu_kernel(x_ref, g_ref, wv_ref, wg_ref, o_ref, h_ref):
    @pl.when(pl.program_id(1) == 0)
    def _():
        _rmsnorm_to(x_ref, g_ref, h_ref)

    h = h_ref[...]
    val = jnp.dot(h, wv_ref[...], preferred_element_type=F32)
    gate = jnp.dot(h, wg_ref[...], preferred_element_type=F32)
    o_ref[...] = (val * jax.nn.sigmoid(gate)).astype(o_ref.dtype)


def _gates_kernel(x_ref, g_ref, w_ref, b_ref, o_ref, h_ref):
    @pl.when(pl.program_id(1) == 0)
    def _():
        _rmsnorm_to(x_ref, g_ref, h_ref)

    z = jnp.dot(h_ref[...], w_ref[...], preferred_element_type=F32)
    o_ref[...] = jax.nn.sigmoid(z + b_ref[...]).astype(o_ref.dtype)


def _in_proj(x, g_mix3, w_in, b_gate3, qkv_scale, l, widths, tm, tn):
    t, d = x.shape
    qkv_w, conv_ch, gate_w = widths
    tm = _tile(t, tm)
    x_spec = pl.BlockSpec((tm, d), lambda i, j: (i, 0))
    g_spec = pl.BlockSpec((None, 1, d), lambda i, j: (l, 0, 0))
    h_scr = pltpu.VMEM((tm, d), BF16)

    def w_spec(width, col0):
        assert col0 % width == 0
        return pl.BlockSpec((None, d, width), lambda i, j: (l, 0, col0 // width + j))

    tq = _tile(qkv_w, tn)
    qkv = pl.pallas_call(
        _qkv_kernel,
        out_shape=jax.ShapeDtypeStruct((t, qkv_w), BF16),
        grid=(t // tm, qkv_w // tq),
        in_specs=[x_spec, g_spec, w_spec(tq, 0), pl.BlockSpec((1, tq), lambda i, j: (0, j))],
        out_specs=pl.BlockSpec((tm, tq), lambda i, j: (i, j)),
        scratch_shapes=[h_scr],
        compiler_params=_params(2),
        name="in_proj_qkv",
    )(x, g_mix3, w_in, qkv_scale)

    tc = _tile(conv_ch, tn // 2)
    glu = pl.pallas_call(
        _glu_kernel,
        out_shape=jax.ShapeDtypeStruct((t, conv_ch), F32),
        grid=(t // tm, conv_ch // tc),
        in_specs=[x_spec, g_spec, w_spec(tc, qkv_w), w_spec(tc, qkv_w + conv_ch)],
        out_specs=pl.BlockSpec((tm, tc), lambda i, j: (i, j)),
        scratch_shapes=[h_scr],
        compiler_params=_params(2),
        name="in_proj_glu",
    )(x, g_mix3, w_in, w_in)

    tg = _tile(gate_w, tn)
    gates = pl.pallas_call(
        _gates_kernel,
        out_shape=jax.ShapeDtypeStruct((t, gate_w), BF16),
        grid=(t // tm, gate_w // tg),
        in_specs=[x_spec, g_spec, w_spec(tg, qkv_w + 2 * conv_ch),
                  pl.BlockSpec((None, 1, tg), lambda i, j: (l, 0, j))],
        out_specs=pl.BlockSpec((tm, tg), lambda i, j: (i, j)),
        scratch_shapes=[h_scr],
        compiler_params=_params(2),
        name="in_proj_gates",
    )(x, g_mix3, w_in, b_gate3)
    return qkv, glu, gates


TW = 256
ATTN_SCORE_BYTES = 24 * 1024 * 1024


def _attn_kernel(slopes_ref, lq_ref, lk_ref, gcol_ref, q_ref, k_ref, v_ref, o_ref,
                 tab_ref, vt_ref, qbd_ref, st_ref, pt_ref, acc_ref, mb_ref, *, lam_init, head_dim, rk, n_sub):
    hd = head_dim
    vd = 2 * hd
    h = pl.program_id(0)
    b = pl.program_id(1)
    qi = pl.program_id(2)
    s_len = k_ref.shape[0]
    n_chunks = s_len // rk
    nt = 2 * TW

    @pl.when((b == 0) & (qi == 0))
    def _():
        neg = -(slopes_ref[h] * LOG2E)
        base = (lax.broadcasted_iota(jnp.int32, (rk, TW), 0)
                - lax.broadcasted_iota(jnp.int32, (rk, TW), 1) - (s_len - TW))
        n_rows = tab_ref.shape[0]
        for r0 in range(0, n_rows, rk):
            rw = min(rk, n_rows - r0)
            tab_ref[r0:r0 + rw, :] = neg * jnp.abs(base[:rw] + r0).astype(F32)

    @pl.when(qi == 0)
    def _():
        for r0 in range(0, s_len, rk):
            vt_ref[:, r0:r0 + rk] = v_ref[r0:r0 + rk, :].T

    lq = lq_ref[...]
    lk = lk_ref[...]
    lam = (jnp.exp(jnp.sum(lq[0:1] * lk[0:1], axis=-1, keepdims=True))
           - jnp.exp(jnp.sum(lq[1:2] * lk[1:2], axis=-1, keepdims=True)) + lam_init)

    zero = jnp.zeros((TW, hd), q_ref.dtype)
    for u in range(n_sub):
        qbd_ref[u, 0:TW, 0:hd] = q_ref[u * TW:(u + 1) * TW, 0:hd]
        qbd_ref[u, 0:TW, hd:vd] = zero
        qbd_ref[u, TW:nt, 0:hd] = zero
        qbd_ref[u, TW:nt, hd:vd] = q_ref[u * TW:(u + 1) * TW, hd:vd]

    def fold(x):
        return x.reshape(rk // SUBLANES, SUBLANES, x.shape[-1])

    def scores(u, j, mx):
        rows = slice(j * rk, (j + 1) * rk)
        off = s_len - TW - (qi * n_sub * TW + u * TW)
        sc = lax.dot_general(k_ref[rows, :], qbd_ref[u], (((1,), (1,)), ((), ())),
                             preferred_element_type=F32)
        tabv = tab_ref[pl.ds(pl.multiple_of(off + j * rk, TW), rk), :]
        mx = list(mx)
        for m in range(2):
            t = sc[:, m * TW:(m + 1) * TW] + tabv
            st_ref[u, rows, m * TW:(m + 1) * TW] = t
            for r in range(0, rk, SUBLANES):
                mx[m] = jnp.maximum(mx[m], t[r:r + SUBLANES])
        return tuple(mx)

    def finish_max(u, mx):
        for m in range(2):
            mb_ref[u, :, m * TW:(m + 1) * TW] = jnp.broadcast_to(
                jnp.max(mx[m], axis=0, keepdims=True), (SUBLANES, TW))

    def exps(u, j, ls):
        rows = slice(j * rk, (j + 1) * rk)
        p = jnp.exp2(fold(st_ref[u, rows, :]) - mb_ref[u][None])
        pt_ref[u, rows, :] = p.reshape(rk, nt).astype(pt_ref.dtype)
        return ls + jnp.sum(p, axis=0)

    def pv(u, j):
        keys = slice(j * rk, (j + 1) * rk)
        acc_ref[u] += jnp.dot(vt_ref[:, keys], pt_ref[u, keys, :], preferred_element_type=F32)

    def finish(u, ls):
        l = jnp.sum(ls, axis=0, keepdims=True)
        ot2 = acc_ref[u]
        ot = ot2[:, :TW] * (1.0 / l[:, :TW]) - ot2[:, TW:] * (lam / l[:, TW:])
        yt = ot * lax.rsqrt(jnp.mean(ot * ot, axis=0, keepdims=True) + NORM_EPS)
        yt = (yt * gcol_ref[...]) * (1.0 - lam_init)
        o_ref[u * TW:(u + 1) * TW, :] = yt.T.astype(o_ref.dtype)

    neg_inf = jnp.full((SUBLANES, TW), -jnp.inf, F32)
    acc_ref[...] = jnp.zeros_like(acc_ref)

    maxima = [(neg_inf, neg_inf)] * n_sub
    sums = [jnp.zeros((SUBLANES, nt), F32)] * n_sub
    for s in range(n_sub + 2):
        for j in range(n_chunks):
            if s < n_sub:
                maxima[s] = scores(s, j, maxima[s])
            if 0 <= s - 1 < n_sub:
                sums[s - 1] = exps(s - 1, j, sums[s - 1])
            if 0 <= s - 2 < n_sub:
                pv(s - 2, j)
        if s < n_sub:
            finish_max(s, maxima[s])
    for u in range(n_sub):
        finish(u, sums[u])


def _attention(qkv, slopes, lambda_q, lambda_k, g_subln_col, l, lam_init, n_seq, s_len, n_heads, head_dim, rk):
    v_dim = 2 * head_dim
    n_sub = max(2, min(4, s_len // TW, ATTN_SCORE_BYTES // (s_len * 2 * TW * 6)))
    tq = n_sub * TW
    assert s_len % tq == 0 and v_dim == TW
    rk = _tile(s_len, rk)
    nq = s_len // tq
    kern = functools.partial(_attn_kernel, lam_init=lam_init, head_dim=head_dim, rk=rk, n_sub=n_sub)
    return pl.pallas_call(
        kern,
        out_shape=jax.ShapeDtypeStruct((n_seq * s_len, n_heads * v_dim), BF16),
        grid=(n_heads, n_seq, nq),
        in_specs=[
            pl.BlockSpec(memory_space=pltpu.SMEM),
            pl.BlockSpec((None, 2, head_dim), lambda h, b, i: (l, 0, 0)),
            pl.BlockSpec((None, 2, head_dim), lambda h, b, i: (l, 0, 0)),
            pl.BlockSpec((None, v_dim, 1), lambda h, b, i: (l, 0, 0)),
            pl.BlockSpec((tq, v_dim), lambda h, b, i: (b * nq + i, h)),
            pl.BlockSpec((s_len, v_dim), lambda h, b, i: (b, n_heads + h)),
            pl.BlockSpec((s_len, v_dim), lambda h, b, i: (b, 2 * n_heads + h)),
        ],
        out_specs=pl.BlockSpec((tq, v_dim), lambda h, b, i: (b * nq + i, h)),
        scratch_shapes=[
            pltpu.VMEM((2 * s_len - TW, TW), F32),
            pltpu.VMEM((v_dim, s_len), BF16),
            pltpu.VMEM((n_sub, 2 * TW, v_dim), BF16),
            pltpu.VMEM((n_sub, s_len, 2 * TW), F32),
            pltpu.VMEM((n_sub, s_len, 2 * TW), BF16),
            pltpu.VMEM((n_sub, v_dim, 2 * TW), F32),
            pltpu.VMEM((n_sub, SUBLANES, 2 * TW), F32),
        ],
        compiler_params=pltpu.CompilerParams(dimension_semantics=("arbitrary",) * 3,
                                             vmem_limit_bytes=VMEM_LIMIT),
        name="diff_attention",
    )(slopes, lambda_q, lambda_k, g_subln_col, qkv, qkv, qkv)


def _conv_kernel(prev_ref, main_ref, next_ref, w_ref, bdw_ref, gln_ref, bln_ref, o_ref, win_ref, sh_ref, acc_ref,
                 *, tt, n_taps, tiles_per_seq, rows, lanes):
    pos = pl.program_id(0) % tiles_per_seq
    win_ref[0:HALO, :] = jnp.where(pos == 0, 0.0, prev_ref[...])
    win_ref[HALO:HALO + tt, :] = main_ref[...]
    win_ref[HALO + tt:, :] = jnp.where(pos == tiles_per_seq - 1, 0.0, next_ref[...])

    span = sh_ref.shape[1]
    for s in range(1, SUBLANES):
        sh_ref[s - 1] = win_ref[s:s + span, :]

    first = HALO - (n_taps - 1) // 2
    def lane_chunk(ci, carry):
        cols = pl.ds(pl.multiple_of(ci * lanes, lanes), lanes)
        for r0 in range(0, tt, rows):
            acc = None
            for s in range(SUBLANES):
                offs = [d for d in range(first, first + n_taps) if d % SUBLANES == s]
                if not offs:
                    continue
                lo, hi = offs[0] - s, offs[-1] - s
                span_rows = slice(r0 + lo, r0 + hi + rows)
                slab = win_ref[span_rows, cols] if s == 0 else sh_ref[s - 1, span_rows, cols]
                for d in offs:
                    k = d - first
                    term = w_ref[k:k + 1, cols] * slab[d - s - lo:d - s - lo + rows]
                    acc = term if acc is None else acc + term
            acc_ref[r0:r0 + rows, cols] = acc + bdw_ref[:, cols]
        return carry

    lax.fori_loop(0, main_ref.shape[1] // lanes, lane_chunk, 0)

    y = acc_ref[...]
    yc = y - jnp.mean(y, axis=-1, keepdims=True)
    yn = yc * lax.rsqrt(jnp.mean(yc * yc, axis=-1, keepdims=True) + LN_EPS)
    z = yn * gln_ref[...] + bln_ref[...]
    o_ref[...] = (z * jax.nn.sigmoid(z)).astype(o_ref.dtype)


def _conv_module(glu, w_dw, b_dw3, g_ln3, b_ln3, l, s_len, tt):
    t, c = glu.shape
    n_taps = w_dw.shape[1]
    pad = (n_taps - 1) // 2
    assert pad <= HALO
    tt = _tile(s_len, tt)
    assert tt % HALO == 0 and t % s_len == 0
    hb = tt // HALO
    n_hblk = t // HALO
    span = tt + (HALO + pad) // SUBLANES * SUBLANES
    vec = pl.BlockSpec((None, 1, c), lambda i: (l, 0, 0))
    kern = functools.partial(_conv_kernel, tt=tt, n_taps=n_taps, tiles_per_seq=s_len // tt, rows=64, lanes=LANES)
    return pl.pallas_call(
        kern,
        out_shape=jax.ShapeDtypeStruct((t, c), BF16),
        grid=(t // tt,),
        in_specs=[
            pl.BlockSpec((HALO, c), lambda i: (jnp.maximum(i * hb - 1, 0), 0)),
            pl.BlockSpec((tt, c), lambda i: (i, 0)),
            pl.BlockSpec((HALO, c), lambda i: (jnp.minimum((i + 1) * hb, n_hblk - 1), 0)),
            pl.BlockSpec((None, n_taps, c), lambda i: (l, 0, 0)),
            vec, vec, vec,
        ],
        out_specs=pl.BlockSpec((tt, c), lambda i: (i, 0)),
        scratch_shapes=[pltpu.VMEM((tt + 2 * HALO, c), F32),
                        pltpu.VMEM((SUBLANES - 1, span, c), F32),
                        pltpu.VMEM((tt, c), F32)],
        compiler_params=_params(1),
        name="conv_module",
    )(glu, glu, glu, w_dw, b_dw3, g_ln3, b_ln3)


def _merge_kernel(a_ref, c_ref, wa_ref, wc_ref, bc_ref, ga_ref, gc_ref, o_ref):
    a = jnp.dot(a_ref[...], wa_ref[...], preferred_element_type=F32)
    c = jnp.dot(c_ref[...], wc_ref[...], preferred_element_type=F32) + bc_ref[...]
    o_ref[...] = (ga_ref[...] * a + gc_ref[...] * c).astype(o_ref.dtype)


def _merge(attn, conv, w_attn_proj, w_conv_proj, b_conv_proj3, gates, l, tm, tn):
    t, kv = attn.shape
    kc = conv.shape[1]
    d = w_attn_proj.shape[2]
    tm = _tile(t, tm)
    tn = _tile(d, tn)
    nj = d // tn
    return pl.pallas_call(
        _merge_kernel,
        out_shape=jax.ShapeDtypeStruct((t, d), BF16),
        grid=(t // tm, nj),
        in_specs=[
            pl.BlockSpec((tm, kv), lambda i, j: (i, 0)),
            pl.BlockSpec((tm, kc), lambda i, j: (i, 0)),
            pl.BlockSpec((None, kv, tn), lambda i, j: (l, 0, j)),
            pl.BlockSpec((None, kc, tn), lambda i, j: (l, 0, j)),
            pl.BlockSpec((None, 1, tn), lambda i, j: (l, 0, j)),
            pl.BlockSpec((tm, tn), lambda i, j: (i, j)),
            pl.BlockSpec((tm, tn), lambda i, j: (i, nj + j)),
        ],
        out_specs=pl.BlockSpec((tm, tn), lambda i, j: (i, j)),
        compiler_params=_params(2),
        name="branch_merge",
    )(attn, conv, w_attn_proj, w_conv_proj, b_conv_proj3, gates, gates)


def _residual_matmul_kernel(a_ref, w_ref, x_ref, o_ref):
    o_ref[...] = x_ref[...] + jnp.dot(a_ref[...], w_ref[...], preferred_element_type=F32)


def _residual_matmul(a, w, x, l, tm, tn, name):
    t, k = a.shape
    d = w.shape[2]
    tm = _tile(t, tm)
    tn = _tile(d, tn)
    return pl.pallas_call(
        _residual_matmul_kernel,
        out_shape=jax.ShapeDtypeStruct((t, d), F32),
        grid=(t // tm, d // tn),
        in_specs=[
            pl.BlockSpec((tm, k), lambda i, j: (i, 0)),
            pl.BlockSpec((None, k, tn), lambda i, j: (l, 0, j)),
            pl.BlockSpec((tm, tn), lambda i, j: (i, j)),
        ],
        out_specs=pl.BlockSpec((tm, tn), lambda i, j: (i, j)),
        compiler_params=_params(2),
        name=name,
    )(a, w, x)


def _out_proj_kernel(a_ref, w_ref, x_ref, g_ref, o_ref, h_ref):
    y = x_ref[...] + jnp.dot(a_ref[...], w_ref[...], preferred_element_type=F32)
    o_ref[...] = y
    yn = y * lax.rsqrt(jnp.mean(y * y, axis=-1, keepdims=True) + NORM_EPS)
    h_ref[...] = (yn * g_ref[...]).astype(h_ref.dtype)


def _out_proj(a, w, x, g_ffn3, l, tm):
    t, k = a.shape
    d = w.shape[2]
    tm = _tile(t, tm)
    return pl.pallas_call(
        _out_proj_kernel,
        out_shape=(jax.ShapeDtypeStruct((t, d), F32), jax.ShapeDtypeStruct((t, d), BF16)),
        grid=(t // tm,),
        in_specs=[
            pl.BlockSpec((tm, k), lambda i: (i, 0)),
            pl.BlockSpec((None, k, d), lambda i: (l, 0, 0)),
            pl.BlockSpec((tm, d), lambda i: (i, 0)),
            pl.BlockSpec((None, 1, d), lambda i: (l, 0, 0)),
        ],
        out_specs=(pl.BlockSpec((tm, d), lambda i: (i, 0)), pl.BlockSpec((tm, d), lambda i: (i, 0))),
        compiler_params=_params(1),
        name="out_proj",
    )(a, w, x, g_ffn3)


def _ffn_in_kernel(h_ref, wg_ref, wu_ref, o_ref, wgb_ref, wub_ref):
    @pl.when(pl.program_id(1) == 0)
    def _():
        wgb_ref[...] = wg_ref[...].astype(wgb_ref.dtype)
        wub_ref[...] = wu_ref[...].astype(wub_ref.dtype)

    h = h_ref[...]
    gate = jnp.dot(h, wgb_ref[...], preferred_element_type=F32)
    up = jnp.dot(h, wub_ref[...], preferred_element_type=F32)
    o_ref[...] = ((gate * jax.nn.sigmoid(gate)) * up).astype(o_ref.dtype)


def _ffn_in(h, w_ffn_in, l, tm, tn):
    t, d = h.shape
    d_ff = w_ffn_in.shape[2] // 2
    tm = _tile(t, tm)
    tn = _tile(d_ff, tn)
    nj = d_ff // tn
    return pl.pallas_call(
        _ffn_in_kernel,
        out_shape=jax.ShapeDtypeStruct((t, d_ff), BF16),
        grid=(nj, t // tm),
        in_specs=[
            pl.BlockSpec((tm, d), lambda j, i: (i, 0)),
            pl.BlockSpec((None, d, tn), lambda j, i: (l, 0, j)),
            pl.BlockSpec((None, d, tn), lambda j, i: (l, 0, nj + j)),
        ],
        out_specs=pl.BlockSpec((tm, tn), lambda j, i: (i, j)),
        scratch_shapes=[pltpu.VMEM((d, tn), BF16), pltpu.VMEM((d, tn), BF16)],
        compiler_params=_params(2),
        name="ffn_in",
    )(h, w_ffn_in, w_ffn_in)


def _final_norm_kernel(x_ref, g_ref, o_ref):
    _rmsnorm_to(x_ref, g_ref, o_ref)


def _final_norm(x, g2, tm):
    t, d = x.shape
    tm = _tile(t, tm)
    return pl.pallas_call(
        _final_norm_kernel,
        out_shape=jax.ShapeDtypeStruct((t, d), F32),
        grid=(t // tm,),
        in_specs=[pl.BlockSpec((tm, d), lambda i: (i, 0)),
                  pl.BlockSpec((1, d), lambda i: (0, 0))],
        out_specs=pl.BlockSpec((tm, d), lambda i: (i, 0)),
        compiler_params=_params(1),
        name="final_norm",
    )(x, g2)


def _lambda_init(layer):
    return 0.8 - 0.6 * float(np.exp(-0.3 * layer))


def kernel(x_prompt, x_sample, g_mix, w_in, b_gate, lambda_q, lambda_k, g_subln, w_attn_proj, w_dw, b_dw,
           g_conv_ln, b_conv_ln, w_conv_proj, b_conv_proj, w_out, g_ffn, w_ffn_in, w_ffn_out, g_final):
    d = x_prompt.shape[2]
    depth = w_in.shape[0]
    head_dim = lambda_q.shape[2]
    v_width = w_attn_proj.shape[1]
    n_heads = v_width // (2 * head_dim)
    qk_width = n_heads * 2 * head_dim
    conv_ch = w_conv_proj.shape[1]
    gate_w = b_gate.shape[1]
    qkv_w = 2 * qk_width + v_width
    assert w_in.shape[2] == qkv_w + 2 * conv_ch + gate_w and gate_w == 2 * d

    slopes = jnp.asarray(2.0 ** (-8.0 * np.arange(1, n_heads + 1) / n_heads), dtype=F32)
    qkv_scale = jnp.asarray(np.concatenate([np.full(qk_width, head_dim ** -0.5 * LOG2E),
                                            np.ones(qkv_w - qk_width)])[None, :], dtype=F32)

    def row(p):
        return p.reshape(p.shape[0], 1, p.shape[1])

    w_in_b = w_in.astype(BF16)
    w_attn_b = w_attn_proj.astype(BF16)
    w_conv_b = w_conv_proj.astype(BF16)
    w_out_b = w_out.astype(BF16)
    w_ffn_out_b = w_ffn_out.astype(BF16)
    w_dw3 = w_dw.reshape(depth, w_dw.shape[1], conv_ch)
    g_mix3, b_gate3, b_dw3 = row(g_mix), row(b_gate), row(b_dw)
    g_subln_col = g_subln.reshape(depth, g_subln.shape[1], 1)
    g_ln3, b_ln3, b_pw3, g_ffn3 = row(g_conv_ln), row(b_conv_ln), row(b_conv_proj), row(g_ffn)
    g2 = g_final.reshape(1, d)

    def trunk(x3):
        n_seq, s_len, _ = x3.shape
        x = x3.reshape(n_seq * s_len, d)
        for l in range(depth):
            lam_init = _lambda_init(l)
            qkv, glu, gates = _in_proj(x, g_mix3, w_in_b, b_gate3, qkv_scale, l, (qkv_w, conv_ch, gate_w),
                                       tm=1024, tn=1024)
            attn = _attention(qkv, slopes, lambda_q, lambda_k, g_subln_col, l, lam_init, n_seq, s_len, n_heads,
                              head_dim, rk=512)
            conv = _conv_module(glu, w_dw3, b_dw3, g_ln3, b_ln3, l, s_len, tt=256)
            merged = _merge(attn, conv, w_attn_b, w_conv_b, b_pw3, gates, l, tm=1024, tn=512)
            x, h_ffn = _out_proj(merged, w_out_b, x, g_ffn3, l, tm=512)
            act = _ffn_in(h_ffn, w_ffn_in, l, tm=1024, tn=512)
            x = _residual_matmul(act, w_ffn_out_b, x, l, tm=1024, tn=512, name="ffn_out")
        return _final_norm(x, g2, tm=512).reshape(x3.shape)

    return (trunk(x_prompt), trunk(x_sample))
```

```python
import functools

import numpy as np
import jax
import jax.numpy as jnp
from jax import lax
from jax.experimental import pallas as pl
from jax.experimental.pallas import tpu as pltpu

NORM_EPS = 1e-6
LN_EPS = 1e-5
LOG2E = 1.4426950408889634
SUBLANES = 8
LANES = 128
HALO = 16
VMEM_LIMIT = 52 * 1024 * 1024

F32 = jnp.float32
BF16 = jnp.bfloat16


def _params(n_axes):
    sem = ("parallel",) + ("arbitrary",) * (n_axes - 1)
    return pltpu.CompilerParams(dimension_semantics=sem, vmem_limit_bytes=VMEM_LIMIT)


def _tile(n, want):
    t = min(n, want)
    assert n % t == 0, (n, want)
    return t


def _rmsnorm_to(x_ref, g_ref, h_ref):
    xf = x_ref[...]
    y = xf * lax.rsqrt(jnp.mean(xf * xf, axis=-1, keepdims=True) + NORM_EPS)
    h_ref[...] = (y * g_ref[...]).astype(h_ref.dtype)


def _qkv_kernel(x_ref, g_ref, w_ref, cs_ref, o_ref, h_ref):
    @pl.when(pl.program_id(1) == 0)
    def _():
        _rmsnorm_to(x_ref, g_ref, h_ref)

    z = jnp.dot(h_ref[...], w_ref[...], preferred_element_type=F32)
    o_ref[...] = (z * cs_ref[...]).astype(o_ref.dtype)


def _glu_kernel(h_ref, wv_ref, wg_ref, o_ref, wvb_ref, wgb_ref):
    @pl.when(pl.program_id(1) == 0)
    def _():
        wvb_ref[...] = wv_ref[...].astype(wvb_ref.dtype)
        wgb_ref[...] = wg_ref[...].astype(wgb_ref.dtype)

    h = h_ref[...]
    val = jnp.dot(h, wvb_ref[...], preferred_element_type=F32)
    gate = jnp.dot(h, wgb_ref[...], preferred_element_type=F32)
    o_ref[...] = (val * jax.nn.sigmoid(gate)).astype(o_ref.dtype)


def _gates_kernel(h_ref, w_ref, b_ref, o_ref, wb_ref):
    @pl.when(pl.program_id(1) == 0)
    def _():
        wb_ref[...] = w_ref[...].astype(wb_ref.dtype)

    z = jnp.dot(h_ref[...], wb_ref[...], preferred_element_type=F32)
    o_ref[...] = jax.nn.sigmoid(z + b_ref[...]).astype(o_ref.dtype)


def _in_proj(x, g_mix3, w_qkv, w_in, b_gate3, qkv_scale, l, widths, tm, tn):
    t, d = x.shape
    qkv_w, conv_ch, gate_w = widths
    tm = _tile(t, tm)
    tq = _tile(qkv_w, tn)
    qkv, h = pl.pallas_call(
        _qkv_kernel,
        out_shape=(jax.ShapeDtypeStruct((t, qkv_w), BF16), jax.ShapeDtypeStruct((t, d), BF16)),
        grid=(t // tm, qkv_w // tq),
        in_specs=[pl.BlockSpec((tm, d), lambda i, j: (i, 0)),
                  pl.BlockSpec((None, 1, d), lambda i, j: (l, 0, 0)),
                  pl.BlockSpec((None, d, tq), lambda i, j: (l, 0, j)),
                  pl.BlockSpec((1, tq), lambda i, j: (0, j))],
        out_specs=(pl.BlockSpec((tm, tq), lambda i, j: (i, j)), pl.BlockSpec((tm, d), lambda i, j: (i, 0))),
        compiler_params=_params(2),
        name="in_proj_qkv",
    )(x, g_mix3, w_qkv, qkv_scale)

    h_spec = pl.BlockSpec((tm, d), lambda j, i: (i, 0))

    def w_spec(width, col0):
        assert col0 % width == 0
        return pl.BlockSpec((None, d, width), lambda j, i: (l, 0, col0 // width + j))

    tc = _tile(conv_ch, tn // 2)
    glu = pl.pallas_call(
        _glu_kernel,
        out_shape=jax.ShapeDtypeStruct((t, conv_ch), F32),
        grid=(conv_ch // tc, t // tm),
        in_specs=[h_spec, w_spec(tc, qkv_w), w_spec(tc, qkv_w + conv_ch)],
        out_specs=pl.BlockSpec((tm, tc), lambda j, i: (i, j)),
        scratch_shapes=[pltpu.VMEM((d, tc), BF16), pltpu.VMEM((d, tc), BF16)],
        compiler_params=_params(2),
        name="in_proj_glu",
    )(h, w_in, w_in)

    tg = _tile(gate_w, tn)
    gates = pl.pallas_call(
        _gates_kernel,
        out_shape=jax.ShapeDtypeStruct((t, gate_w), BF16),
        grid=(gate_w // tg, t // tm),
        in_specs=[h_spec, w_spec(tg, qkv_w + 2 * conv_ch),
                  pl.BlockSpec((None, 1, tg), lambda j, i: (l, 0, j))],
        out_specs=pl.BlockSpec((tm, tg), lambda j, i: (i, j)),
        scratch_shapes=[pltpu.VMEM((d, tg), BF16)],
        compiler_params=_params(2),
        name="in_proj_gates",
    )(h, w_in, b_gate3)
    return qkv, glu, gates


TW = 256
ATTN_SCORE_BYTES = 24 * 1024 * 1024


def _attn_kernel(slopes_ref, lq_ref, lk_ref, gcol_ref, q_ref, k_ref, v_ref, o_ref,
                 tab_ref, vt_ref, qbd_ref, st_ref, pt_ref, acc_ref, mb_ref, *, lam_init, head_dim, rk, n_sub):
    hd = head_dim
    vd = 2 * hd
    h = pl.program_id(0)
    b = pl.program_id(1)
    qi = pl.program_id(2)
    s_len = k_ref.shape[0]
    n_chunks = s_len // rk
    nt = 2 * TW

    @pl.when((b == 0) & (qi == 0))
    def _():
        neg = -(slopes_ref[h] * LOG2E)
        base = (lax.broadcasted_iota(jnp.int32, (rk, TW), 0)
                - lax.broadcasted_iota(jnp.int32, (rk, TW), 1) - (s_len - TW))
        n_rows = tab_ref.shape[0]
        for r0 in range(0, n_rows, rk):
            rw = min(rk, n_rows - r0)
            tab_ref[r0:r0 + rw, :] = neg * jnp.abs(base[:rw] + r0).astype(F32)

    @pl.when(qi == 0)
    def _():
        for r0 in range(0, s_len, rk):
            vt_ref[:, r0:r0 + rk] = v_ref[r0:r0 + rk, :].T

    lq = lq_ref[...]
    lk = lk_ref[...]
    lam = (jnp.exp(jnp.sum(lq[0:1] * lk[0:1], axis=-1, keepdims=True))
           - jnp.exp(jnp.sum(lq[1:2] * lk[1:2], axis=-1, keepdims=True)) + lam_init)

    zero = jnp.zeros((TW, hd), q_ref.dtype)
    for u in range(n_sub):
        qbd_ref[u, 0:TW, 0:hd] = q_ref[u * TW:(u + 1) * TW, 0:hd]
        qbd_ref[u, 0:TW, hd:vd] = zero
        qbd_ref[u, TW:nt, 0:hd] = zero
        qbd_ref[u, TW:nt, hd:vd] = q_ref[u * TW:(u + 1) * TW, hd:vd]

    def fold(x):
        return x.reshape(rk // SUBLANES, SUBLANES, x.shape[-1])

    def scores(u, j, mx):
        rows = slice(j * rk, (j + 1) * rk)
        off = s_len - TW - (qi * n_sub * TW + u * TW)
        sc = lax.dot_general(k_ref[rows, :], qbd_ref[u], (((1,), (1,)), ((), ())),
                             preferred_element_type=F32)
        tabv = tab_ref[pl.ds(pl.multiple_of(off + j * rk, TW), rk), :]
        mx = list(mx)
        for m in range(2):
            t = sc[:, m * TW:(m + 1) * TW] + tabv
            st_ref[u, rows, m * TW:(m + 1) * TW] = t
            for r in range(0, rk, SUBLANES):
                mx[m] = jnp.maximum(mx[m], t[r:r + SUBLANES])
        return tuple(mx)

    def finish_max(u, mx):
        for m in range(2):
            mb_ref[u, :, m * TW:(m + 1) * TW] = jnp.broadcast_to(
                jnp.max(mx[m], axis=0, keepdims=True), (SUBLANES, TW))

    def exps(u, j, ls):
        rows = slice(j * rk, (j + 1) * rk)
        p = jnp.exp2(fold(st_ref[u, rows, :]) - mb_ref[u][None])
        pt_ref[u, rows, :] = p.reshape(rk, nt).astype(pt_ref.dtype)
        return ls + jnp.sum(p, axis=0)

    def pv(u, j):
        keys = slice(j * rk, (j + 1) * rk)
        acc_ref[u] += jnp.dot(vt_ref[:, keys], pt_ref[u, keys, :], preferred_element_type=F32)

    def finish(u, ls):
        l = jnp.sum(ls, axis=0, keepdims=True)
        ot2 = acc_ref[u]
        ot = ot2[:, :TW] * (1.0 / l[:, :TW]) - ot2[:, TW:] * (lam / l[:, TW:])
        yt = ot * lax.rsqrt(jnp.mean(ot * ot, axis=0, keepdims=True) + NORM_EPS)
        yt = (yt * gcol_ref[...]) * (1.0 - lam_init)
        o_ref[u * TW:(u + 1) * TW, :] = yt.T.astype(o_ref.dtype)

    neg_inf = jnp.full((SUBLANES, TW), -jnp.inf, F32)
    acc_ref[...] = jnp.zeros_like(acc_ref)

    maxima = [(neg_inf, neg_inf)] * n_sub
    sums = [jnp.zeros((SUBLANES, nt), F32)] * n_sub
    for s in range(n_sub + 2):
        for j in range(n_chunks):
            if s < n_sub:
                maxima[s] = scores(s, j, maxima[s])
            if 0 <= s - 1 < n_sub:
                sums[s - 1] = exps(s - 1, j, sums[s - 1])
            if 0 <= s - 2 < n_sub:
                pv(s - 2, j)
        if s < n_sub:
            finish_max(s, maxima[s])
    for u in range(n_sub):
        finish(u, sums[u])


def _attention(qkv, slopes, lambda_q, lambda_k, g_subln_col, l, lam_init, n_seq, s_len, n_heads, head_dim, rk):
    v_dim = 2 * head_dim
    n_sub = max(2, min(4, s_len // TW, ATTN_SCORE_BYTES // (s_len * 2 * TW * 6)))
    tq = n_sub * TW
    assert s_len % tq == 0 and v_dim == TW
    rk = _tile(s_len, rk)
    nq = s_len // tq
    kern = functools.partial(_attn_kernel, lam_init=lam_init, head_dim=head_dim, rk=rk, n_sub=n_sub)
    return pl.pallas_call(
        kern,
        out_shape=jax.ShapeDtypeStruct((n_seq * s_len, n_heads * v_dim), BF16),
        grid=(n_heads, n_seq, nq),
        in_specs=[
            pl.BlockSpec(memory_space=pltpu.SMEM),
            pl.BlockSpec((None, 2, head_dim), lambda h, b, i: (l, 0, 0)),
            pl.BlockSpec((None, 2, head_dim), lambda h, b, i: (l, 0, 0)),
            pl.BlockSpec((None, v_dim, 1), lambda h, b, i: (l, 0, 0)),
            pl.BlockSpec((tq, v_dim), lambda h, b, i: (b * nq + i, h)),
            pl.BlockSpec((s_len, v_dim), lambda h, b, i: (b, n_heads + h)),
            pl.BlockSpec((s_len, v_dim), lambda h, b, i: (b, 2 * n_heads + h)),
        ],
        out_specs=pl.BlockSpec((tq, v_dim), lambda h, b, i: (b * nq + i, h)),
        scratch_shapes=[
            pltpu.VMEM((2 * s_len - TW, TW), F32),
            pltpu.VMEM((v_dim, s_len), BF16),
            pltpu.VMEM((n_sub, 2 * TW, v_dim), BF16),
            pltpu.VMEM((n_sub, s_len, 2 * TW), F32),
            pltpu.VMEM((n_sub, s_len, 2 * TW), BF16),
            pltpu.VMEM((n_sub, v_dim, 2 * TW), F32),
            pltpu.VMEM((n_sub, SUBLANES, 2 * TW), F32),
        ],
        compiler_params=pltpu.CompilerParams(dimension_semantics=("arbitrary",) * 3,
                                             vmem_limit_bytes=VMEM_LIMIT),
        name="diff_attention",
    )(slopes, lambda_q, lambda_k, g_subln_col, qkv, qkv, qkv)


def _conv_kernel(prev_ref, main_ref, next_ref, w_ref, bdw_ref, gln_ref, bln_ref, o_ref, win_ref, sh_ref, acc_ref,
                 *, tt, n_taps, tiles_per_seq, rows, lanes):
    pos = pl.program_id(0) % tiles_per_seq
    win_ref[0:HALO, :] = jnp.where(pos == 0, 0.0, prev_ref[...])
    win_ref[HALO:HALO + tt, :] = main_ref[...]
    win_ref[HALO + tt:, :] = jnp.where(pos == tiles_per_seq - 1, 0.0, next_ref[...])

    span = sh_ref.shape[1]
    for s in range(1, SUBLANES):
        sh_ref[s - 1] = win_ref[s:s + span, :]

    first = HALO - (n_taps - 1) // 2
    def lane_chunk(ci, carry):
        cols = pl.ds(pl.multiple_of(ci * lanes, lanes), lanes)
        for r0 in range(0, tt, rows):
            acc = None
            for s in range(SUBLANES):
                offs = [d for d in range(first, first + n_taps) if d % SUBLANES == s]
                if not offs:
                    continue
                lo, hi = offs[0] - s, offs[-1] - s
                span_rows = slice(r0 + lo, r0 + hi + rows)
                slab = win_ref[span_rows, cols] if s == 0 else sh_ref[s - 1, span_rows, cols]
                for d in offs:
                    k = d - first
                    term = w_ref[k:k + 1, cols] * slab[d - s - lo:d - s - lo + rows]
                    acc = term if acc is None else acc + term
            acc_ref[r0:r0 + rows, cols] = acc + bdw_ref[:, cols]
        return carry

    lax.fori_loop(0, main_ref.shape[1] // lanes, lane_chunk, 0)

    y = acc_ref[...]
    yc = y - jnp.mean(y, axis=-1, keepdims=True)
    yn = yc * lax.rsqrt(jnp.mean(yc * yc, axis=-1, keepdims=True) + LN_EPS)
    z = yn * gln_ref[...] + bln_ref[...]
    o_ref[...] = (z * jax.nn.sigmoid(z)).astype(o_ref.dtype)


def _conv_module(glu, w_dw, b_dw3, g_ln3, b_ln3, l, s_len, tt):
    t, c = glu.shape
    n_taps = w_dw.shape[1]
    pad = (n_taps - 1) // 2
    assert pad <= HALO
    tt = _tile(s_len, tt)
    assert tt % HALO == 0 and t % s_len == 0
    hb = tt // HALO
    n_hblk = t // HALO
    span = tt + (HALO + pad) // SUBLANES * SUBLANES
    vec = pl.BlockSpec((None, 1, c), lambda i: (l, 0, 0))
    kern = functools.partial(_conv_kernel, tt=tt, n_taps=n_taps, tiles_per_seq=s_len // tt, rows=64, lanes=LANES)
    return pl.pallas_call(
        kern,
        out_shape=jax.ShapeDtypeStruct((t, c), BF16),
        grid=(t // tt,),
        in_specs=[
            pl.BlockSpec((HALO, c), lambda i: (jnp.maximum(i * hb - 1, 0), 0)),
            pl.BlockSpec((tt, c), lambda i: (i, 0)),
            pl.BlockSpec((HALO, c), lambda i: (jnp.minimum((i + 1) * hb, n_hblk - 1), 0)),
            pl.BlockSpec((None, n_taps, c), lambda i: (l, 0, 0)),
            vec, vec, vec,
        ],
        out_specs=pl.BlockSpec((tt, c), lambda i: (i, 0)),
        scratch_shapes=[pltpu.VMEM((tt + 2 * HALO, c), F32),
                        pltpu.VMEM((SUBLANES - 1, span, c), F32),
                        pltpu.VMEM((tt, c), F32)],
        compiler_params=_params(1),
        name="conv_module",
    )(glu, glu, glu, w_dw, b_dw3, g_ln3, b_ln3)


def _merge_kernel(a_ref, c_ref, wa_ref, wc_ref, bc_ref, ga_ref, gc_ref, o_ref):
    a = jnp.dot(a_ref[...], wa_ref[...], preferred_element_type=F32)
    c = jnp.dot(c_ref[...], wc_ref[...], preferred_element_type=F32) + bc_ref[...]
    o_ref[...] = (ga_ref[...] * a + gc_ref[...] * c).astype(o_ref.dtype)


def _merge(attn, conv, w_attn_proj, w_conv_proj, b_conv_proj3, gates, l, tm, tn):
    t, kv = attn.shape
    kc = conv.shape[1]
    d = w_attn_proj.shape[2]
    tm = _tile(t, tm)
    tn = _tile(d, tn)
    nj = d // tn
    return pl.pallas_call(
        _merge_kernel,
        out_shape=jax.ShapeDtypeStruct((t, d), BF16),
        grid=(t // tm, nj),
        in_specs=[
            pl.BlockSpec((tm, kv), lambda i, j: (i, 0)),
            pl.BlockSpec((tm, kc), lambda i, j: (i, 0)),
            pl.BlockSpec((None, kv, tn), lambda i, j: (l, 0, j)),
            pl.BlockSpec((None, kc, tn), lambda i, j: (l, 0, j)),
            pl.BlockSpec((None, 1, tn), lambda i, j: (l, 0, j)),
            pl.BlockSpec((tm, tn), lambda i, j: (i, j)),
            pl.BlockSpec((tm, tn), lambda i, j: (i, nj + j)),
        ],
        out_specs=pl.BlockSpec((tm, tn), lambda i, j: (i, j)),
        compiler_params=_params(2),
        name="branch_merge",
    )(attn, conv, w_attn_proj, w_conv_proj, b_conv_proj3, gates, gates)


def _residual_matmul_kernel(a_ref, w_ref, x_ref, o_ref):
    o_ref[...] = x_ref[...] + jnp.dot(a_ref[...], w_ref[...], preferred_element_type=F32)


def _residual_matmul(a, w, x, l, tm, tn, name):
    t, k = a.shape
    d = w.shape[2]
    tm = _tile(t, tm)
    tn = _tile(d, tn)
    return pl.pallas_call(
        _residual_matmul_kernel,
        out_shape=jax.ShapeDtypeStruct((t, d), F32),
        grid=(t // tm, d // tn),
        in_specs=[
            pl.BlockSpec((tm, k), lambda i, j: (i, 0)),
            pl.BlockSpec((None, k, tn), lambda i, j: (l, 0, j)),
            pl.BlockSpec((tm, tn), lambda i, j: (i, j)),
        ],
        out_specs=pl.BlockSpec((tm, tn), lambda i, j: (i, j)),
        compiler_params=_params(2),
        name=name,
    )(a, w, x)


def _out_proj_kernel(a_ref, w_ref, x_ref, g_ref, o_ref, h_ref):
    y = x_ref[...] + jnp.dot(a_ref[...], w_ref[...], preferred_element_type=F32)
    o_ref[...] = y
    yn = y * lax.rsqrt(jnp.mean(y * y, axis=-1, keepdims=True) + NORM_EPS)
    h_ref[...] = (yn * g_ref[...]).astype(h_ref.dtype)


def _out_proj(a, w, x, g_ffn3, l, tm):
    t, k = a.shape
    d = w.shape[2]
    tm = _tile(t, tm)
    return pl.pallas_call(
        _out_proj_kernel,
        out_shape=(jax.ShapeDtypeStruct((t, d), F32), jax.ShapeDtypeStruct((t, d), BF16)),
        grid=(t // tm,),
        in_specs=[
            pl.BlockSpec((tm, k), lambda i: (i, 0)),
            pl.BlockSpec((None, k, d), lambda i: (l, 0, 0)),
            pl.BlockSpec((tm, d), lambda i: (i, 0)),
            pl.BlockSpec((None, 1, d), lambda i: (l, 0, 0)),
        ],
        out_specs=(pl.BlockSpec((tm, d), lambda i: (i, 0)), pl.BlockSpec((tm, d), lambda i: (i, 0))),
        compiler_params=_params(1),
        name="out_proj",
    )(a, w, x, g_ffn3)


def _ffn_in_kernel(h_ref, wg_ref, wu_ref, o_ref, wgb_ref, wub_ref):
    @pl.when(pl.program_id(1) == 0)
    def _():
        wgb_ref[...] = wg_ref[...].astype(wgb_ref.dtype)
        wub_ref[...] = wu_ref[...].astype(wub_ref.dtype)

    h = h_ref[...]
    gate = jnp.dot(h, wgb_ref[...], preferred_element_type=F32)
    up = jnp.dot(h, wub_ref[...], preferred_element_type=F32)
    o_ref[...] = ((gate * jax.nn.sigmoid(gate)) * up).astype(o_ref.dtype)


def _ffn_in(h, w_ffn_in, l, tm, tn):
    t, d = h.shape
    d_ff = w_ffn_in.shape[2] // 2
    tm = _tile(t, tm)
    tn = _tile(d_ff, tn)
    nj = d_ff // tn
    return pl.pallas_call(
        _ffn_in_kernel,
        out_shape=jax.ShapeDtypeStruct((t, d_ff), BF16),
        grid=(nj, t // tm),
        in_specs=[
            pl.BlockSpec((tm, d), lambda j, i: (i, 0)),
            pl.BlockSpec((None, d, tn), lambda j, i: (l, 0, j)),
            pl.BlockSpec((None, d, tn), lambda j, i: (l, 0, nj + j)),
        ],
        out_specs=pl.BlockSpec((tm, tn), lambda j, i: (i, j)),
        scratch_shapes=[pltpu.VMEM((d, tn), BF16), pltpu.VMEM((d, tn), BF16)],
        compiler_params=_params(2),
        name="ffn_in",
    )(h, w_ffn_in, w_ffn_in)


def _final_norm_kernel(x_ref, g_ref, o_ref):
    _rmsnorm_to(x_ref, g_ref, o_ref)


def _final_norm(x, g2, tm):
    t, d = x.shape
    tm = _tile(t, tm)
    return pl.pallas_call(
        _final_norm_kernel,
        out_shape=jax.ShapeDtypeStruct((t, d), F32),
        grid=(t // tm,),
        in_specs=[pl.BlockSpec((tm, d), lambda i: (i, 0)),
                  pl.BlockSpec((1, d), lambda i: (0, 0))],
        out_specs=pl.BlockSpec((tm, d), lambda i: (i, 0)),
        compiler_params=_params(1),
        name="final_norm",
    )(x, g2)


def _lambda_init(layer):
    return 0.8 - 0.6 * float(np.exp(-0.3 * layer))


def kernel(x_prompt, x_sample, g_mix, w_in, b_gate, lambda_q, lambda_k, g_subln, w_attn_proj, w_dw, b_dw,
           g_conv_ln, b_conv_ln, w_conv_proj, b_conv_proj, w_out, g_ffn, w_ffn_in, w_ffn_out, g_final):
    d = x_prompt.shape[2]
    depth = w_in.shape[0]
    head_dim = lambda_q.shape[2]
    v_width = w_attn_proj.shape[1]
    n_heads = v_width // (2 * head_dim)
    qk_width = n_heads * 2 * head_dim
    conv_ch = w_conv_proj.shape[1]
    gate_w = b_gate.shape[1]
    qkv_w = 2 * qk_width + v_width
    assert w_in.shape[2] == qkv_w + 2 * conv_ch + gate_w and gate_w == 2 * d

    slopes = jnp.asarray(2.0 ** (-8.0 * np.arange(1, n_heads + 1) / n_heads), dtype=F32)
    qkv_scale = jnp.asarray(np.concatenate([np.full(qk_width, head_dim ** -0.5 * LOG2E),
                                            np.ones(qkv_w - qk_width)])[None, :], dtype=F32)

    def row(p):
        return p.reshape(p.shape[0], 1, p.shape[1])

    w_qkv_b = w_in[:, :, :qkv_w].astype(BF16)
    w_attn_b = w_attn_proj.astype(BF16)
    w_conv_b = w_conv_proj.astype(BF16)
    w_out_b = w_out.astype(BF16)
    w_ffn_out_b = w_ffn_out.astype(BF16)
    w_dw3 = w_dw.reshape(depth, w_dw.shape[1], conv_ch)
    g_mix3, b_gate3, b_dw3 = row(g_mix), row(b_gate), row(b_dw)
    g_subln_col = g_subln.reshape(depth, g_subln.shape[1], 1)
    g_ln3, b_ln3, b_pw3, g_ffn3 = row(g_conv_ln), row(b_conv_ln), row(b_conv_proj), row(g_ffn)
    g2 = g_final.reshape(1, d)

    def trunk(x3):
        n_seq, s_len, _ = x3.shape
        x = x3.reshape(n_seq * s_len, d)
        for l in range(depth):
            lam_init = _lambda_init(l)
            qkv, glu, gates = _in_proj(x, g_mix3, w_qkv_b, w_in, b_gate3, qkv_scale, l, (qkv_w, conv_ch, gate_w),
                                       tm=1024, tn=1024)
            attn = _attention(qkv, slopes, lambda_q, lambda_k, g_subln_col, l, lam_init, n_seq, s_len, n_heads,
                              head_dim, rk=512)
            conv = _conv_module(glu, w_dw3, b_dw3, g_ln3, b_ln3, l, s_len, tt=256)
            merged = _merge(attn, conv, w_attn_b, w_conv_b, b_pw3, gates, l, tm=1024, tn=512)
            x, h_ffn = _out_proj(merged, w_out_b, x, g_ffn3, l, tm=512)
            act = _ffn_in(h_ffn, w_ffn_in, l, tm=1024, tn=512)
            x = _residual_matmul(act, w_ffn_out_b, x, l, tm=1024, tn=512, name="ffn_out")
        return _final_norm(x, g2, tm=512).reshape(x3.shape)

    return (trunk(x_prompt), trunk(x_sample))
```

```python
import functools

import numpy as np
import jax
import jax.numpy as jnp
from jax import lax
from jax.experimental import pallas as pl
from jax.experimental.pallas import tpu as pltpu

NORM_EPS = 1e-6
LN_EPS = 1e-5
LOG2E = 1.4426950408889634
SUBLANES = 8
LANES = 128
HALO = 16
VMEM_LIMIT = 52 * 1024 * 1024

F32 = jnp.float32
BF16 = jnp.bfloat16

TILES = {
    "in_proj": (1024, 1024),
    "merge": (1024, 512),
    "out_proj_rows": 512,
    "ffn_in": (1024, 512),
    "ffn_out": (1024, 512),
    "conv_rows": 256,
    "norm_rows": 512,
    "cast_rows": 256,
    "attn_key_chunks": 8,
}


def _params(n_axes):
    sem = ("parallel",) + ("arbitrary",) * (n_axes - 1)
    return pltpu.CompilerParams(dimension_semantics=sem, vmem_limit_bytes=VMEM_LIMIT)


def _tile(n, want):
    t = min(n, want)
    assert n % t == 0, (n, want)
    return t


def _cast_kernel(w_ref, o_ref):
    o_ref[...] = w_ref[...].astype(o_ref.dtype)


def _leading_columns_bf16(w, n_cols, tr):
    depth, k, _ = w.shape
    tr = _tile(k, tr)
    return pl.pallas_call(
        _cast_kernel,
        out_shape=jax.ShapeDtypeStruct((depth, k, n_cols), BF16),
        grid=(depth, k // tr),
        in_specs=[pl.BlockSpec((None, tr, n_cols), lambda l, r: (l, r, 0))],
        out_specs=pl.BlockSpec((None, tr, n_cols), lambda l, r: (l, r, 0)),
        compiler_params=_params(2),
        name="cast_qkv_weights",
    )(w)


def _rmsnorm_to(x_ref, g_ref, h_ref):
    xf = x_ref[...]
    y = xf * lax.rsqrt(jnp.mean(xf * xf, axis=-1, keepdims=True) + NORM_EPS)
    h_ref[...] = (y * g_ref[...]).astype(h_ref.dtype)


def _qkv_kernel(x_ref, g_ref, w_ref, cs_ref, o_ref, h_ref):
    @pl.when(pl.program_id(1) == 0)
    def _():
        _rmsnorm_to(x_ref, g_ref, h_ref)

    z = jnp.dot(h_ref[...], w_ref[...], preferred_element_type=F32)
    o_ref[...] = (z * cs_ref[...]).astype(o_ref.dtype)


def _glu_kernel(h_ref, wv_ref, wg_ref, o_ref, wvb_ref, wgb_ref):
    @pl.when(pl.program_id(1) == 0)
    def _():
        wvb_ref[...] = wv_ref[...].astype(wvb_ref.dtype)
        wgb_ref[...] = wg_ref[...].astype(wgb_ref.dtype)

    h = h_ref[...]
    val = jnp.dot(h, wvb_ref[...], preferred_element_type=F32)
    gate = jnp.dot(h, wgb_ref[...], preferred_element_type=F32)
    o_ref[...] = (val * jax.nn.sigmoid(gate)).astype(o_ref.dtype)


def _gates_kernel(h_ref, w_ref, b_ref, o_ref, wb_ref):
    @pl.when(pl.program_id(1) == 0)
    def _():
        wb_ref[...] = w_ref[...].astype(wb_ref.dtype)

    z = jnp.dot(h_ref[...], wb_ref[...], preferred_element_type=F32)
    o_ref[...] = jax.nn.sigmoid(z + b_ref[...]).astype(o_ref.dtype)


def _in_proj(x, g_mix3, w_qkv, w_in, b_gate3, qkv_scale, l, widths, tm, tn):
    t, d = x.shape
    qkv_w, conv_ch, gate_w = widths
    tm = _tile(t, tm)
    tq = _tile(qkv_w, tn)
    qkv, h = pl.pallas_call(
        _qkv_kernel,
        out_shape=(jax.ShapeDtypeStruct((t, qkv_w), BF16), jax.ShapeDtypeStruct((t, d), BF16)),
        grid=(t // tm, qkv_w // tq),
        in_specs=[pl.BlockSpec((tm, d), lambda i, j: (i, 0)),
                  pl.BlockSpec((None, 1, d), lambda i, j: (l, 0, 0)),
                  pl.BlockSpec((None, d, tq), lambda i, j: (l, 0, j)),
                  pl.BlockSpec((1, tq), lambda i, j: (0, j))],
        out_specs=(pl.BlockSpec((tm, tq), lambda i, j: (i, j)), pl.BlockSpec((tm, d), lambda i, j: (i, 0))),
        compiler_params=_params(2),
        name="in_proj_qkv",
    )(x, g_mix3, w_qkv, qkv_scale)

    h_spec = pl.BlockSpec((tm, d), lambda j, i: (i, 0))

    def w_spec(width, col0):
        assert col0 % width == 0
        return pl.BlockSpec((None, d, width), lambda j, i: (l, 0, col0 // width + j))

    tc = _tile(conv_ch, tn // 2)
    glu = pl.pallas_call(
        _glu_kernel,
        out_shape=jax.ShapeDtypeStruct((t, conv_ch), F32),
        grid=(conv_ch // tc, t // tm),
        in_specs=[h_spec, w_spec(tc, qkv_w), w_spec(tc, qkv_w + conv_ch)],
        out_specs=pl.BlockSpec((tm, tc), lambda j, i: (i, j)),
        scratch_shapes=[pltpu.VMEM((d, tc), BF16), pltpu.VMEM((d, tc), BF16)],
        compiler_params=_params(2),
        name="in_proj_glu",
    )(h, w_in, w_in)

    tg = _tile(gate_w, tn)
    gates = pl.pallas_call(
        _gates_kernel,
        out_shape=jax.ShapeDtypeStruct((t, gate_w), BF16),
        grid=(gate_w // tg, t // tm),
        in_specs=[h_spec, w_spec(tg, qkv_w + 2 * conv_ch),
                  pl.BlockSpec((None, 1, tg), lambda j, i: (l, 0, j))],
        out_specs=pl.BlockSpec((tm, tg), lambda j, i: (i, j)),
        scratch_shapes=[pltpu.VMEM((d, tg), BF16)],
        compiler_params=_params(2),
        name="in_proj_gates",
    )(h, w_in, b_gate3)
    return qkv, glu, gates


TW = 256
ATTN_SCORE_BYTES = 24 * 1024 * 1024


def _attn_kernel(slopes_ref, lq_ref, lk_ref, gcol_ref, q_ref, k_ref, v_ref, o_ref,
                 tab_ref, vt_ref, qbd_ref, st_ref, pt_ref, acc_ref, mb_ref, *, lam_init, head_dim, rk, n_sub):
    hd = head_dim
    vd = 2 * hd
    h = pl.program_id(0)
    b = pl.program_id(1)
    qi = pl.program_id(2)
    s_len = k_ref.shape[0]
    n_chunks = s_len // rk
    nt = 2 * TW

    @pl.when((b == 0) & (qi == 0))
    def _():
        neg = -(slopes_ref[h] * LOG2E)
        base = (lax.broadcasted_iota(jnp.int32, (rk, TW), 0)
                - lax.broadcasted_iota(jnp.int32, (rk, TW), 1) - (s_len - TW))
        n_rows = tab_ref.shape[0]
        for r0 in range(0, n_rows, rk):
            rw = min(rk, n_rows - r0)
            tab_ref[r0:r0 + rw, :] = neg * jnp.abs(base[:rw] + r0).astype(F32)

    @pl.when(qi == 0)
    def _():
        for r0 in range(0, s_len, rk):
            vt_ref[:, r0:r0 + rk] = v_ref[r0:r0 + rk, :].T

    lq = lq_ref[...]
    lk = lk_ref[...]
    lam = (jnp.exp(jnp.sum(lq[0:1] * lk[0:1], axis=-1, keepdims=True))
           - jnp.exp(jnp.sum(lq[1:2] * lk[1:2], axis=-1, keepdims=True)) + lam_init)

    zero = jnp.zeros((TW, hd), q_ref.dtype)
    for u in range(n_sub):
        qbd_ref[u, 0:TW, 0:hd] = q_ref[u * TW:(u + 1) * TW, 0:hd]
        qbd_ref[u, 0:TW, hd:vd] = zero
        qbd_ref[u, TW:nt, 0:hd] = zero
        qbd_ref[u, TW:nt, hd:vd] = q_ref[u * TW:(u + 1) * TW, hd:vd]

    def fold(x):
        return x.reshape(rk // SUBLANES, SUBLANES, x.shape[-1])

    def scores(u, j, mx):
        rows = slice(j * rk, (j + 1) * rk)
        off = s_len - TW - (qi * n_sub * TW + u * TW)
        sc = lax.dot_general(k_ref[rows, :], qbd_ref[u], (((1,), (1,)), ((), ())),
                             preferred_element_type=F32)
        tabv = tab_ref[pl.ds(pl.multiple_of(off + j * rk, TW), rk), :]
        mx = list(mx)
        for m in range(2):
            t = sc[:, m * TW:(m + 1) * TW] + tabv
            st_ref[u, rows, m * TW:(m + 1) * TW] = t
            for r in range(0, rk, SUBLANES):
                mx[m] = jnp.maximum(mx[m], t[r:r + SUBLANES])
        return tuple(mx)

    def finish_max(u, mx):
        for m in range(2):
            mb_ref[u, :, m * TW:(m + 1) * TW] = jnp.broadcast_to(
                jnp.max(mx[m], axis=0, keepdims=True), (SUBLANES, TW))

    def exps(u, j, ls):
        rows = slice(j * rk, (j + 1) * rk)
        p = jnp.exp2(fold(st_ref[u, rows, :]) - mb_ref[u][None])
        pt_ref[u, rows, :] = p.reshape(rk, nt).astype(pt_ref.dtype)
        return ls + jnp.sum(p, axis=0)

    def pv(u, j):
        keys = slice(j * rk, (j + 1) * rk)
        acc_ref[u] += jnp.dot(vt_ref[:, keys], pt_ref[u, keys, :], preferred_element_type=F32)

    def finish(u, ls):
        l = jnp.sum(ls, axis=0, keepdims=True)
        ot2 = acc_ref[u]
        ot = ot2[:, :TW] * (1.0 / l[:, :TW]) - ot2[:, TW:] * (lam / l[:, TW:])
        yt = ot * lax.rsqrt(jnp.mean(ot * ot, axis=0, keepdims=True) + NORM_EPS)
        yt = (yt * gcol_ref[...]) * (1.0 - lam_init)
        o_ref[u * TW:(u + 1) * TW, :] = yt.T.astype(o_ref.dtype)

    neg_inf = jnp.full((SUBLANES, TW), -jnp.inf, F32)
    acc_ref[...] = jnp.zeros_like(acc_ref)

    maxima = [(neg_inf, neg_inf)] * n_sub
    sums = [jnp.zeros((SUBLANES, nt), F32)] * n_sub
    for s in range(n_sub + 2):
        for j in range(n_chunks):
            if s < n_sub:
                maxima[s] = scores(s, j, maxima[s])
            if 0 <= s - 1 < n_sub:
                sums[s - 1] = exps(s - 1, j, sums[s - 1])
            if 0 <= s - 2 < n_sub:
                pv(s - 2, j)
        if s < n_sub:
            finish_max(s, maxima[s])
    for u in range(n_sub):
        finish(u, sums[u])


def _attention(qkv, slopes, lambda_q, lambda_k, g_subln_col, l, lam_init, n_seq, s_len, n_heads, head_dim, rk):
    v_dim = 2 * head_dim
    n_sub = max(2, min(4, s_len // TW, ATTN_SCORE_BYTES // (s_len * 2 * TW * 6)))
    tq = n_sub * TW
    assert s_len % tq == 0 and v_dim == TW
    rk = _tile(s_len, rk)
    nq = s_len // tq
    kern = functools.partial(_attn_kernel, lam_init=lam_init, head_dim=head_dim, rk=rk, n_sub=n_sub)
    return pl.pallas_call(
        kern,
        out_shape=jax.ShapeDtypeStruct((n_seq * s_len, n_heads * v_dim), BF16),
        grid=(n_heads, n_seq, nq),
        in_specs=[
            pl.BlockSpec(memory_space=pltpu.SMEM),
            pl.BlockSpec((None, 2, head_dim), lambda h, b, i: (l, 0, 0)),
            pl.BlockSpec((None, 2, head_dim), lambda h, b, i: (l, 0, 0)),
            pl.BlockSpec((None, v_dim, 1), lambda h, b, i: (l, 0, 0)),
            pl.BlockSpec((tq, v_dim), lambda h, b, i: (b * nq + i, h)),
            pl.BlockSpec((s_len, v_dim), lambda h, b, i: (b, n_heads + h)),
            pl.BlockSpec((s_len, v_dim), lambda h, b, i: (b, 2 * n_heads + h)),
        ],
        out_specs=pl.BlockSpec((tq, v_dim), lambda h, b, i: (b * nq + i, h)),
        scratch_shapes=[
            pltpu.VMEM((2 * s_len - TW, TW), F32),
            pltpu.VMEM((v_dim, s_len), BF16),
            pltpu.VMEM((n_sub, 2 * TW, v_dim), BF16),
            pltpu.VMEM((n_sub, s_len, 2 * TW), F32),
            pltpu.VMEM((n_sub, s_len, 2 * TW), BF16),
            pltpu.VMEM((n_sub, v_dim, 2 * TW), F32),
            pltpu.VMEM((n_sub, SUBLANES, 2 * TW), F32),
        ],
        compiler_params=pltpu.CompilerParams(dimension_semantics=("arbitrary",) * 3,
                                             vmem_limit_bytes=VMEM_LIMIT),
        name="diff_attention",
    )(slopes, lambda_q, lambda_k, g_subln_col, qkv, qkv, qkv)


def _conv_kernel(prev_ref, main_ref, next_ref, w_ref, bdw_ref, gln_ref, bln_ref, o_ref, win_ref, sh_ref, acc_ref,
                 *, tt, n_taps, tiles_per_seq, rows, lanes):
    pos = pl.program_id(0) % tiles_per_seq
    win_ref[0:HALO, :] = jnp.where(pos == 0, 0.0, prev_ref[...])
    win_ref[HALO:HALO + tt, :] = main_ref[...]
    win_ref[HALO + tt:, :] = jnp.where(pos == tiles_per_seq - 1, 0.0, next_ref[...])

    span = sh_ref.shape[1]
    for s in range(1, SUBLANES):
        sh_ref[s - 1] = win_ref[s:s + span, :]

    first = HALO - (n_taps - 1) // 2
    def lane_chunk(ci, carry):
        cols = pl.ds(pl.multiple_of(ci * lanes, lanes), lanes)
        for r0 in range(0, tt, rows):
            acc = None
            for s in range(SUBLANES):
                offs = [d for d in range(first, first + n_taps) if d % SUBLANES == s]
                if not offs:
                    continue
                lo, hi = offs[0] - s, offs[-1] - s
                span_rows = slice(r0 + lo, r0 + hi + rows)
                slab = win_ref[span_rows, cols] if s == 0 else sh_ref[s - 1, span_rows, cols]
                for d in offs:
                    k = d - first
                    term = w_ref[k:k + 1, cols] * slab[d - s - lo:d - s - lo + rows]
                    acc = term if acc is None else acc + term
            acc_ref[r0:r0 + rows, cols] = acc + bdw_ref[:, cols]
        return carry

    lax.fori_loop(0, main_ref.shape[1] // lanes, lane_chunk, 0)

    y = acc_ref[...]
    yc = y - jnp.mean(y, axis=-1, keepdims=True)
    yn = yc * lax.rsqrt(jnp.mean(yc * yc, axis=-1, keepdims=True) + LN_EPS)
    z = yn * gln_ref[...] + bln_ref[...]
    o_ref[...] = (z * jax.nn.sigmoid(z)).astype(o_ref.dtype)


def _conv_module(glu, w_dw, b_dw3, g_ln3, b_ln3, l, s_len, tt):
    t, c = glu.shape
    n_taps = w_dw.shape[1]
    pad = (n_taps - 1) // 2
    assert pad <= HALO
    tt = _tile(s_len, tt)
    assert tt % HALO == 0 and t % s_len == 0
    hb = tt // HALO
    n_hblk = t // HALO
    span = tt + (HALO + pad) // SUBLANES * SUBLANES
    vec = pl.BlockSpec((None, 1, c), lambda i: (l, 0, 0))
    kern = functools.partial(_conv_kernel, tt=tt, n_taps=n_taps, tiles_per_seq=s_len // tt, rows=64, lanes=LANES)
    return pl.pallas_call(
        kern,
        out_shape=jax.ShapeDtypeStruct((t, c), BF16),
        grid=(t // tt,),
        in_specs=[
            pl.BlockSpec((HALO, c), lambda i: (jnp.maximum(i * hb - 1, 0), 0)),
            pl.BlockSpec((tt, c), lambda i: (i, 0)),
            pl.BlockSpec((HALO, c), lambda i: (jnp.minimum((i + 1) * hb, n_hblk - 1), 0)),
            pl.BlockSpec((None, n_taps, c), lambda i: (l, 0, 0)),
            vec, vec, vec,
        ],
        out_specs=pl.BlockSpec((tt, c), lambda i: (i, 0)),
        scratch_shapes=[pltpu.VMEM((tt + 2 * HALO, c), F32),
                        pltpu.VMEM((SUBLANES - 1, span, c), F32),
                        pltpu.VMEM((tt, c), F32)],
        compiler_params=_params(1),
        name="conv_module",
    )(glu, glu, glu, w_dw, b_dw3, g_ln3, b_ln3)


def _merge_kernel(a_ref, c_ref, wa_ref, wc_ref, bc_ref, ga_ref, gc_ref, o_ref):
    a = jnp.dot(a_ref[...], wa_ref[...], preferred_element_type=F32)
    c = jnp.dot(c_ref[...], wc_ref[...], preferred_element_type=F32) + bc_ref[...]
    o_ref[...] = (ga_ref[...] * a + gc_ref[...] * c).astype(o_ref.dtype)


def _merge(attn, conv, w_attn_proj, w_conv_proj, b_conv_proj3, gates, l, tm, tn):
    t, kv = attn.shape
    kc = conv.shape[1]
    d = w_attn_proj.shape[2]
    tm = _tile(t, tm)
    tn = _tile(d, tn)
    nj = d // tn
    return pl.pallas_call(
        _merge_kernel,
        out_shape=jax.ShapeDtypeStruct((t, d), BF16),
        grid=(t // tm, nj),
        in_specs=[
            pl.BlockSpec((tm, kv), lambda i, j: (i, 0)),
            pl.BlockSpec((tm, kc), lambda i, j: (i, 0)),
            pl.BlockSpec((None, kv, tn), lambda i, j: (l, 0, j)),
            pl.BlockSpec((None, kc, tn), lambda i, j: (l, 0, j)),
            pl.BlockSpec((None, 1, tn), lambda i, j: (l, 0, j)),
            pl.BlockSpec((tm, tn), lambda i, j: (i, j)),
            pl.BlockSpec((tm, tn), lambda i, j: (i, nj + j)),
        ],
        out_specs=pl.BlockSpec((tm, tn), lambda i, j: (i, j)),
        compiler_params=_params(2),
        name="branch_merge",
    )(attn, conv, w_attn_proj, w_conv_proj, b_conv_proj3, gates, gates)


def _residual_matmul_kernel(a_ref, w_ref, x_ref, o_ref):
    o_ref[...] = x_ref[...] + jnp.dot(a_ref[...], w_ref[...], preferred_element_type=F32)


def _residual_matmul(a, w, x, l, tm, tn, name):
    t, k = a.shape
    d = w.shape[2]
    tm = _tile(t, tm)
    tn = _tile(d, tn)
    return pl.pallas_call(
        _residual_matmul_kernel,
        out_shape=jax.ShapeDtypeStruct((t, d), F32),
        grid=(t // tm, d // tn),
        in_specs=[
            pl.BlockSpec((tm, k), lambda i, j: (i, 0)),
            pl.BlockSpec((None, k, tn), lambda i, j: (l, 0, j)),
            pl.BlockSpec((tm, tn), lambda i, j: (i, j)),
        ],
        out_specs=pl.BlockSpec((tm, tn), lambda i, j: (i, j)),
        compiler_params=_params(2),
        name=name,
    )(a, w, x)


def _out_proj_kernel(a_ref, w_ref, x_ref, g_ref, o_ref, h_ref):
    y = x_ref[...] + jnp.dot(a_ref[...], w_ref[...], preferred_element_type=F32)
    o_ref[...] = y
    yn = y * lax.rsqrt(jnp.mean(y * y, axis=-1, keepdims=True) + NORM_EPS)
    h_ref[...] = (yn * g_ref[...]).astype(h_ref.dtype)


def _out_proj(a, w, x, g_ffn3, l, tm):
    t, k = a.shape
    d = w.shape[2]
    tm = _tile(t, tm)
    return pl.pallas_call(
        _out_proj_kernel,
        out_shape=(jax.ShapeDtypeStruct((t, d), F32), jax.ShapeDtypeStruct((t, d), BF16)),
        grid=(t // tm,),
        in_specs=[
            pl.BlockSpec((tm, k), lambda i: (i, 0)),
            pl.BlockSpec((None, k, d), lambda i: (l, 0, 0)),
            pl.BlockSpec((tm, d), lambda i: (i, 0)),
            pl.BlockSpec((None, 1, d), lambda i: (l, 0, 0)),
        ],
        out_specs=(pl.BlockSpec((tm, d), lambda i: (i, 0)), pl.BlockSpec((tm, d), lambda i: (i, 0))),
        compiler_params=_params(1),
        name="out_proj",
    )(a, w, x, g_ffn3)


def _ffn_in_kernel(h_ref, wg_ref, wu_ref, o_ref, wgb_ref, wub_ref):
    @pl.when(pl.program_id(1) == 0)
    def _():
        wgb_ref[...] = wg_ref[...].astype(wgb_ref.dtype)
        wub_ref[...] = wu_ref[...].astype(wub_ref.dtype)

    h = h_ref[...]
    gate = jnp.dot(h, wgb_ref[...], preferred_element_type=F32)
    up = jnp.dot(h, wub_ref[...], preferred_element_type=F32)
    o_ref[...] = ((gate * jax.nn.sigmoid(gate)) * up).astype(o_ref.dtype)


def _ffn_in(h, w_ffn_in, l, tm, tn):
    t, d = h.shape
    d_ff = w_ffn_in.shape[2] // 2
    tm = _tile(t, tm)
    tn = _tile(d_ff, tn)
    nj = d_ff // tn
    return pl.pallas_call(
        _ffn_in_kernel,
        out_shape=jax.ShapeDtypeStruct((t, d_ff), BF16),
        grid=(nj, t // tm),
        in_specs=[
            pl.BlockSpec((tm, d), lambda j, i: (i, 0)),
            pl.BlockSpec((None, d, tn), lambda j, i: (l, 0, j)),
            pl.BlockSpec((None, d, tn), lambda j, i: (l, 0, nj + j)),
        ],
        out_specs=pl.BlockSpec((tm, tn), lambda j, i: (i, j)),
        scratch_shapes=[pltpu.VMEM((d, tn), BF16), pltpu.VMEM((d, tn), BF16)],
        compiler_params=_params(2),
        name="ffn_in",
    )(h, w_ffn_in, w_ffn_in)


def _final_norm_kernel(x_ref, g_ref, o_ref):
    _rmsnorm_to(x_ref, g_ref, o_ref)


def _final_norm(x, g2, tm):
    t, d = x.shape
    tm = _tile(t, tm)
    return pl.pallas_call(
        _final_norm_kernel,
        out_shape=jax.ShapeDtypeStruct((t, d), F32),
        grid=(t // tm,),
        in_specs=[pl.BlockSpec((tm, d), lambda i: (i, 0)),
                  pl.BlockSpec((1, d), lambda i: (0, 0))],
        out_specs=pl.BlockSpec((tm, d), lambda i: (i, 0)),
        compiler_params=_params(1),
        name="final_norm",
    )(x, g2)


def _lambda_init(layer):
    return 0.8 - 0.6 * float(np.exp(-0.3 * layer))


def kernel(x_prompt, x_sample, g_mix, w_in, b_gate, lambda_q, lambda_k, g_subln, w_attn_proj, w_dw, b_dw,
           g_conv_ln, b_conv_ln, w_conv_proj, b_conv_proj, w_out, g_ffn, w_ffn_in, w_ffn_out, g_final):
    d = x_prompt.shape[2]
    depth = w_in.shape[0]
    head_dim = lambda_q.shape[2]
    v_width = w_attn_proj.shape[1]
    n_heads = v_width // (2 * head_dim)
    qk_width = n_heads * 2 * head_dim
    conv_ch = w_conv_proj.shape[1]
    gate_w = b_gate.shape[1]
    qkv_w = 2 * qk_width + v_width
    assert w_in.shape[2] == qkv_w + 2 * conv_ch + gate_w and gate_w == 2 * d

    slopes = jnp.asarray(2.0 ** (-8.0 * np.arange(1, n_heads + 1) / n_heads), dtype=F32)
    qkv_scale = jnp.asarray(np.concatenate([np.full(qk_width, head_dim ** -0.5 * LOG2E),
                                            np.ones(qkv_w - qk_width)])[None, :], dtype=F32)

    def row(p):
        return p.reshape(p.shape[0], 1, p.shape[1])

    w_qkv_b = _leading_columns_bf16(w_in, qkv_w, tr=TILES["cast_rows"])
    w_attn_b = w_attn_proj.astype(BF16)
    w_conv_b = w_conv_proj.astype(BF16)
    w_out_b = w_out.astype(BF16)
    w_ffn_out_b = w_ffn_out.astype(BF16)
    w_dw3 = w_dw.reshape(depth, w_dw.shape[1], conv_ch)
    g_mix3, b_gate3, b_dw3 = row(g_mix), row(b_gate), row(b_dw)
    g_subln_col = g_subln.reshape(depth, g_subln.shape[1], 1)
    g_ln3, b_ln3, b_pw3, g_ffn3 = row(g_conv_ln), row(b_conv_ln), row(b_conv_proj), row(g_ffn)
    g2 = g_final.reshape(1, d)

    def trunk(x3):
        n_seq, s_len, _ = x3.shape
        x = x3.reshape(n_seq * s_len, d)
        for l in range(depth):
            lam_init = _lambda_init(l)
            qkv, glu, gates = _in_proj(x, g_mix3, w_qkv_b, w_in, b_gate3, qkv_scale, l, (qkv_w, conv_ch, gate_w),
                                       *TILES["in_proj"])
            attn = _attention(qkv, slopes, lambda_q, lambda_k, g_subln_col, l, lam_init, n_seq, s_len, n_heads,
                              head_dim, rk=max(TW, s_len // TILES["attn_key_chunks"]))
            conv = _conv_module(glu, w_dw3, b_dw3, g_ln3, b_ln3, l, s_len, tt=TILES["conv_rows"])
            merged = _merge(attn, conv, w_attn_b, w_conv_b, b_pw3, gates, l, *TILES["merge"])
            x, h_ffn = _out_proj(merged, w_out_b, x, g_ffn3, l, tm=TILES["out_proj_rows"])
            act = _ffn_in(h_ffn, w_ffn_in, l, *TILES["ffn_in"])
            x = _residual_matmul(act, w_ffn_out_b, x, l, *TILES["ffn_out"], name="ffn_out")
        return _final_norm(x, g2, tm=TILES["norm_rows"]).reshape(x3.shape)

    return (trunk(x_prompt), trunk(x_sample))
```

```python
import functools

import numpy as np
import jax
import jax.numpy as jnp
from jax import lax
from jax.experimental import pallas as pl
from jax.experimental.pallas import tpu as pltpu

NORM_EPS = 1e-6
LN_EPS = 1e-5
LOG2E = 1.4426950408889634
SUBLANES = 8
LANES = 128
HALO = 16
VMEM_LIMIT = 52 * 1024 * 1024

F32 = jnp.float32
BF16 = jnp.bfloat16

TILES = {
    "in_proj": (1024, 1024),
    "merge": (1024, 512),
    "out_proj_rows": 512,
    "ffn_in": (1024, 512),
    "ffn_out": (1024, 512),
    "conv_rows": 256,
    "norm_rows": 512,
    "cast_rows": 256,
    "attn_key_chunks": 8,
}


def _params(n_axes):
    sem = ("parallel",) + ("arbitrary",) * (n_axes - 1)
    return pltpu.CompilerParams(dimension_semantics=sem, vmem_limit_bytes=VMEM_LIMIT)


def _tile(n, want):
    t = min(n, want)
    assert n % t == 0, (n, want)
    return t


def _cast_kernel(w_ref, o_ref):
    o_ref[...] = w_ref[...].astype(o_ref.dtype)


def _leading_columns_bf16(w, n_cols, tr):
    depth, k, _ = w.shape
    tr = _tile(k, tr)
    return pl.pallas_call(
        _cast_kernel,
        out_shape=jax.ShapeDtypeStruct((depth, k, n_cols), BF16),
        grid=(depth, k // tr),
        in_specs=[pl.BlockSpec((None, tr, n_cols), lambda l, r: (l, r, 0))],
        out_specs=pl.BlockSpec((None, tr, n_cols), lambda l, r: (l, r, 0)),
        compiler_params=_params(2),
        name="cast_qkv_weights",
    )(w)


def _sigmoid(x):
    return 0.5 * jnp.tanh(0.5 * x) + 0.5


def _rmsnorm_to(x_ref, g_ref, h_ref):
    xf = x_ref[...]
    y = xf * lax.rsqrt(jnp.mean(xf * xf, axis=-1, keepdims=True) + NORM_EPS)
    h_ref[...] = (y * g_ref[...]).astype(h_ref.dtype)


def _qkv_kernel(x_ref, g_ref, w_ref, cs_ref, o_ref, h_ref):
    @pl.when(pl.program_id(1) == 0)
    def _():
        _rmsnorm_to(x_ref, g_ref, h_ref)

    z = jnp.dot(h_ref[...], w_ref[...], preferred_element_type=F32)
    o_ref[...] = (z * cs_ref[...]).astype(o_ref.dtype)


def _glu_kernel(h_ref, wv_ref, wg_ref, o_ref, wvb_ref, wgb_ref):
    @pl.when(pl.program_id(1) == 0)
    def _():
        wvb_ref[...] = wv_ref[...].astype(wvb_ref.dtype)
        wgb_ref[...] = wg_ref[...].astype(wgb_ref.dtype)

    h = h_ref[...]
    val = jnp.dot(h, wvb_ref[...], preferred_element_type=F32)
    gate = jnp.dot(h, wgb_ref[...], preferred_element_type=F32)
    o_ref[...] = (val * _sigmoid(gate)).astype(o_ref.dtype)


def _gates_kernel(h_ref, w_ref, b_ref, o_ref, wb_ref):
    @pl.when(pl.program_id(1) == 0)
    def _():
        wb_ref[...] = w_ref[...].astype(wb_ref.dtype)

    z = jnp.dot(h_ref[...], wb_ref[...], preferred_element_type=F32)
    o_ref[...] = _sigmoid(z + b_ref[...]).astype(o_ref.dtype)


def _in_proj(x, g_mix3, w_qkv, w_in, b_gate3, qkv_scale, l, widths, tm, tn):
    t, d = x.shape
    qkv_w, conv_ch, gate_w = widths
    tm = _tile(t, tm)
    tq = _tile(qkv_w, tn)
    qkv, h = pl.pallas_call(
        _qkv_kernel,
        out_shape=(jax.ShapeDtypeStruct((t, qkv_w), BF16), jax.ShapeDtypeStruct((t, d), BF16)),
        grid=(t // tm, qkv_w // tq),
        in_specs=[pl.BlockSpec((tm, d), lambda i, j: (i, 0)),
                  pl.BlockSpec((None, 1, d), lambda i, j: (l, 0, 0)),
                  pl.BlockSpec((None, d, tq), lambda i, j: (l, 0, j)),
                  pl.BlockSpec((1, tq), lambda i, j: (0, j))],
        out_specs=(pl.BlockSpec((tm, tq), lambda i, j: (i, j)), pl.BlockSpec((tm, d), lambda i, j: (i, 0))),
        compiler_params=_params(2),
        name="in_proj_qkv",
    )(x, g_mix3, w_qkv, qkv_scale)

    h_spec = pl.BlockSpec((tm, d), lambda j, i: (i, 0))

    def w_spec(width, col0):
        assert col0 % width == 0
        return pl.BlockSpec((None, d, width), lambda j, i: (l, 0, col0 // width + j))

    tc = _tile(conv_ch, tn // 2)
    glu = pl.pallas_call(
        _glu_kernel,
        out_shape=jax.ShapeDtypeStruct((t, conv_ch), F32),
        grid=(conv_ch // tc, t // tm),
        in_specs=[h_spec, w_spec(tc, qkv_w), w_spec(tc, qkv_w + conv_ch)],
        out_specs=pl.BlockSpec((tm, tc), lambda j, i: (i, j)),
        scratch_shapes=[pltpu.VMEM((d, tc), BF16), pltpu.VMEM((d, tc), BF16)],
        compiler_params=_params(2),
        name="in_proj_glu",
    )(h, w_in, w_in)

    tg = _tile(gate_w, tn)
    gates = pl.pallas_call(
        _gates_kernel,
        out_shape=jax.ShapeDtypeStruct((t, gate_w), BF16),
        grid=(gate_w // tg, t // tm),
        in_specs=[h_spec, w_spec(tg, qkv_w + 2 * conv_ch),
                  pl.BlockSpec((None, 1, tg), lambda j, i: (l, 0, j))],
        out_specs=pl.BlockSpec((tm, tg), lambda j, i: (i, j)),
        scratch_shapes=[pltpu.VMEM((d, tg), BF16)],
        compiler_params=_params(2),
        name="in_proj_gates",
    )(h, w_in, b_gate3)
    return qkv, glu, gates


TW = 256
ATTN_SCORE_BYTES = 24 * 1024 * 1024


def _attn_kernel(slopes_ref, lq_ref, lk_ref, gcol_ref, q_ref, k_ref, v_ref, o_ref,
                 tab_ref, vt_ref, qbd_ref, st_ref, pt_ref, acc_ref, mb_ref, *, lam_init, head_dim, rk, n_sub):
    hd = head_dim
    vd = 2 * hd
    h = pl.program_id(0)
    b = pl.program_id(1)
    qi = pl.program_id(2)
    s_len = k_ref.shape[0]
    n_chunks = s_len // rk
    nt = 2 * TW

    @pl.when((b == 0) & (qi == 0))
    def _():
        neg = -(slopes_ref[h] * LOG2E)
        base = (lax.broadcasted_iota(jnp.int32, (rk, TW), 0)
                - lax.broadcasted_iota(jnp.int32, (rk, TW), 1) - (s_len - TW))
        n_rows = tab_ref.shape[0]
        for r0 in range(0, n_rows, rk):
            rw = min(rk, n_rows - r0)
            tab_ref[r0:r0 + rw, :] = neg * jnp.abs(base[:rw] + r0).astype(F32)

    @pl.when(qi == 0)
    def _():
        for r0 in range(0, s_len, rk):
            vt_ref[:, r0:r0 + rk] = v_ref[r0:r0 + rk, :].T

    lq = lq_ref[...]
    lk = lk_ref[...]
    lam = (jnp.exp(jnp.sum(lq[0:1] * lk[0:1], axis=-1, keepdims=True))
           - jnp.exp(jnp.sum(lq[1:2] * lk[1:2], axis=-1, keepdims=True)) + lam_init)

    zero = jnp.zeros((TW, hd), q_ref.dtype)
    for u in range(n_sub):
        qbd_ref[u, 0:TW, 0:hd] = q_ref[u * TW:(u + 1) * TW, 0:hd]
        qbd_ref[u, 0:TW, hd:vd] = zero
        qbd_ref[u, TW:nt, 0:hd] = zero
        qbd_ref[u, TW:nt, hd:vd] = q_ref[u * TW:(u + 1) * TW, hd:vd]

    def fold(x):
        return x.reshape(rk // SUBLANES, SUBLANES, x.shape[-1])

    def scores(u, j, mx):
        rows = slice(j * rk, (j + 1) * rk)
        off = s_len - TW - (qi * n_sub * TW + u * TW)
        sc = lax.dot_general(k_ref[rows, :], qbd_ref[u], (((1,), (1,)), ((), ())),
                             preferred_element_type=F32)
        tabv = tab_ref[pl.ds(pl.multiple_of(off + j * rk, TW), rk), :]
        mx = list(mx)
        for m in range(2):
            t = sc[:, m * TW:(m + 1) * TW] + tabv
            st_ref[u, rows, m * TW:(m + 1) * TW] = t
            for r in range(0, rk, SUBLANES):
                mx[m] = jnp.maximum(mx[m], t[r:r + SUBLANES])
        return tuple(mx)

    def finish_max(u, mx):
        for m in range(2):
            mb_ref[u, :, m * TW:(m + 1) * TW] = jnp.broadcast_to(
                jnp.max(mx[m], axis=0, keepdims=True), (SUBLANES, TW))

    def exps(u, j, ls):
        rows = slice(j * rk, (j + 1) * rk)
        p = jnp.exp2(fold(st_ref[u, rows, :]) - mb_ref[u][None])
        pt_ref[u, rows, :] = p.reshape(rk, nt).astype(pt_ref.dtype)
        return ls + jnp.sum(p, axis=0)

    def pv(u, j):
        keys = slice(j * rk, (j + 1) * rk)
        acc_ref[u] += jnp.dot(vt_ref[:, keys], pt_ref[u, keys, :], preferred_element_type=F32)

    def finish(u, ls):
        l = jnp.sum(ls, axis=0, keepdims=True)
        ot2 = acc_ref[u]
        ot = ot2[:, :TW] * (1.0 / l[:, :TW]) - ot2[:, TW:] * (lam / l[:, TW:])
        yt = ot * lax.rsqrt(jnp.mean(ot * ot, axis=0, keepdims=True) + NORM_EPS)
        yt = (yt * gcol_ref[...]) * (1.0 - lam_init)
        o_ref[u * TW:(u + 1) * TW, :] = yt.T.astype(o_ref.dtype)

    neg_inf = jnp.full((SUBLANES, TW), -jnp.inf, F32)
    acc_ref[...] = jnp.zeros_like(acc_ref)

    maxima = [(neg_inf, neg_inf)] * n_sub
    sums = [jnp.zeros((SUBLANES, nt), F32)] * n_sub
    for s in range(n_sub + 2):
        for j in range(n_chunks):
            if s < n_sub:
                maxima[s] = scores(s, j, maxima[s])
            if 0 <= s - 1 < n_sub:
                sums[s - 1] = exps(s - 1, j, sums[s - 1])
            if 0 <= s - 2 < n_sub:
                pv(s - 2, j)
        if s < n_sub:
            finish_max(s, maxima[s])
    for u in range(n_sub):
        finish(u, sums[u])


def _attention(qkv, slopes, lambda_q, lambda_k, g_subln_col, l, lam_init, n_seq, s_len, n_heads, head_dim, rk):
    v_dim = 2 * head_dim
    n_sub = max(2, min(4, s_len // TW, ATTN_SCORE_BYTES // (s_len * 2 * TW * 6)))
    tq = n_sub * TW
    assert s_len % tq == 0 and v_dim == TW
    rk = _tile(s_len, rk)
    nq = s_len // tq
    kern = functools.partial(_attn_kernel, lam_init=lam_init, head_dim=head_dim, rk=rk, n_sub=n_sub)
    return pl.pallas_call(
        kern,
        out_shape=jax.ShapeDtypeStruct((n_seq * s_len, n_heads * v_dim), BF16),
        grid=(n_heads, n_seq, nq),
        in_specs=[
            pl.BlockSpec(memory_space=pltpu.SMEM),
            pl.BlockSpec((None, 2, head_dim), lambda h, b, i: (l, 0, 0)),
            pl.BlockSpec((None, 2, head_dim), lambda h, b, i: (l, 0, 0)),
            pl.BlockSpec((None, v_dim, 1), lambda h, b, i: (l, 0, 0)),
            pl.BlockSpec((tq, v_dim), lambda h, b, i: (b * nq + i, h)),
            pl.BlockSpec((s_len, v_dim), lambda h, b, i: (b, n_heads + h)),
            pl.BlockSpec((s_len, v_dim), lambda h, b, i: (b, 2 * n_heads + h)),
        ],
        out_specs=pl.BlockSpec((tq, v_dim), lambda h, b, i: (b * nq + i, h)),
        scratch_shapes=[
            pltpu.VMEM((2 * s_len - TW, TW), F32),
            pltpu.VMEM((v_dim, s_len), BF16),
            pltpu.VMEM((n_sub, 2 * TW, v_dim), BF16),
            pltpu.VMEM((n_sub, s_len, 2 * TW), F32),
            pltpu.VMEM((n_sub, s_len, 2 * TW), BF16),
            pltpu.VMEM((n_sub, v_dim, 2 * TW), F32),
            pltpu.VMEM((n_sub, SUBLANES, 2 * TW), F32),
        ],
        compiler_params=pltpu.CompilerParams(dimension_semantics=("arbitrary",) * 3,
                                             vmem_limit_bytes=VMEM_LIMIT),
        name="diff_attention",
    )(slopes, lambda_q, lambda_k, g_subln_col, qkv, qkv, qkv)


def _conv_kernel(prev_ref, main_ref, next_ref, w_ref, bdw_ref, gln_ref, bln_ref, o_ref, win_ref, sh_ref, acc_ref,
                 *, tt, n_taps, tiles_per_seq, rows, lanes):
    pos = pl.program_id(0) % tiles_per_seq
    win_ref[0:HALO, :] = jnp.where(pos == 0, 0.0, prev_ref[...])
    win_ref[HALO:HALO + tt, :] = main_ref[...]
    win_ref[HALO + tt:, :] = jnp.where(pos == tiles_per_seq - 1, 0.0, next_ref[...])

    span = sh_ref.shape[1]
    for s in range(1, SUBLANES):
        sh_ref[s - 1] = win_ref[s:s + span, :]

    first = HALO - (n_taps - 1) // 2
    def lane_chunk(ci, carry):
        cols = pl.ds(pl.multiple_of(ci * lanes, lanes), lanes)
        for r0 in range(0, tt, rows):
            acc = None
            for s in range(SUBLANES):
                offs = [d for d in range(first, first + n_taps) if d % SUBLANES == s]
                if not offs:
                    continue
                lo, hi = offs[0] - s, offs[-1] - s
                span_rows = slice(r0 + lo, r0 + hi + rows)
                slab = win_ref[span_rows, cols] if s == 0 else sh_ref[s - 1, span_rows, cols]
                for d in offs:
                    k = d - first
                    term = w_ref[k:k + 1, cols] * slab[d - s - lo:d - s - lo + rows]
                    acc = term if acc is None else acc + term
            acc_ref[r0:r0 + rows, cols] = acc + bdw_ref[:, cols]
        return carry

    lax.fori_loop(0, main_ref.shape[1] // lanes, lane_chunk, 0)

    y = acc_ref[...]
    yc = y - jnp.mean(y, axis=-1, keepdims=True)
    yn = yc * lax.rsqrt(jnp.mean(yc * yc, axis=-1, keepdims=True) + LN_EPS)
    z = yn * gln_ref[...] + bln_ref[...]
    o_ref[...] = (z * _sigmoid(z)).astype(o_ref.dtype)


def _conv_module(glu, w_dw, b_dw3, g_ln3, b_ln3, l, s_len, tt):
    t, c = glu.shape
    n_taps = w_dw.shape[1]
    pad = (n_taps - 1) // 2
    assert pad <= HALO
    tt = _tile(s_len, tt)
    assert tt % HALO == 0 and t % s_len == 0
    hb = tt // HALO
    n_hblk = t // HALO
    span = tt + (HALO + pad) // SUBLANES * SUBLANES
    vec = pl.BlockSpec((None, 1, c), lambda i: (l, 0, 0))
    kern = functools.partial(_conv_kernel, tt=tt, n_taps=n_taps, tiles_per_seq=s_len // tt, rows=64, lanes=LANES)
    return pl.pallas_call(
        kern,
        out_shape=jax.ShapeDtypeStruct((t, c), BF16),
        grid=(t // tt,),
        in_specs=[
            pl.BlockSpec((HALO, c), lambda i: (jnp.maximum(i * hb - 1, 0), 0)),
            pl.BlockSpec((tt, c), lambda i: (i, 0)),
            pl.BlockSpec((HALO, c), lambda i: (jnp.minimum((i + 1) * hb, n_hblk - 1), 0)),
            pl.BlockSpec((None, n_taps, c), lambda i: (l, 0, 0)),
            vec, vec, vec,
        ],
        out_specs=pl.BlockSpec((tt, c), lambda i: (i, 0)),
        scratch_shapes=[pltpu.VMEM((tt + 2 * HALO, c), F32),
                        pltpu.VMEM((SUBLANES - 1, span, c), F32),
                        pltpu.VMEM((tt, c), F32)],
        compiler_params=_params(1),
        name="conv_module",
    )(glu, glu, glu, w_dw, b_dw3, g_ln3, b_ln3)


def _merge_kernel(a_ref, c_ref, wa_ref, wc_ref, bc_ref, ga_ref, gc_ref, o_ref):
    a = jnp.dot(a_ref[...], wa_ref[...], preferred_element_type=F32)
    c = jnp.dot(c_ref[...], wc_ref[...], preferred_element_type=F32) + bc_ref[...]
    o_ref[...] = (ga_ref[...] * a + gc_ref[...] * c).astype(o_ref.dtype)


def _merge(attn, conv, w_attn_proj, w_conv_proj, b_conv_proj3, gates, l, tm, tn):
    t, kv = attn.shape
    kc = conv.shape[1]
    d = w_attn_proj.shape[2]
    tm = _tile(t, tm)
    tn = _tile(d, tn)
    nj = d // tn
    return pl.pallas_call(
        _merge_kernel,
        out_shape=jax.ShapeDtypeStruct((t, d), BF16),
        grid=(t // tm, nj),
        in_specs=[
            pl.BlockSpec((tm, kv), lambda i, j: (i, 0)),
            pl.BlockSpec((tm, kc), lambda i, j: (i, 0)),
            pl.BlockSpec((None, kv, tn), lambda i, j: (l, 0, j)),
            pl.BlockSpec((None, kc, tn), lambda i, j: (l, 0, j)),
            pl.BlockSpec((None, 1, tn), lambda i, j: (l, 0, j)),
            pl.BlockSpec((tm, tn), lambda i, j: (i, j)),
            pl.BlockSpec((tm, tn), lambda i, j: (i, nj + j)),
        ],
        out_specs=pl.BlockSpec((tm, tn), lambda i, j: (i, j)),
        compiler_params=_params(2),
        name="branch_merge",
    )(attn, conv, w_attn_proj, w_conv_proj, b_conv_proj3, gates, gates)


def _residual_matmul_kernel(a_ref, w_ref, x_ref, o_ref):
    o_ref[...] = x_ref[...] + jnp.dot(a_ref[...], w_ref[...], preferred_element_type=F32)


def _residual_matmul(a, w, x, l, tm, tn, name):
    t, k = a.shape
    d = w.shape[2]
    tm = _tile(t, tm)
    tn = _tile(d, tn)
    return pl.pallas_call(
        _residual_matmul_kernel,
        out_shape=jax.ShapeDtypeStruct((t, d), F32),
        grid=(t // tm, d // tn),
        in_specs=[
            pl.BlockSpec((tm, k), lambda i, j: (i, 0)),
            pl.BlockSpec((None, k, tn), lambda i, j: (l, 0, j)),
            pl.BlockSpec((tm, tn), lambda i, j: (i, j)),
        ],
        out_specs=pl.BlockSpec((tm, tn), lambda i, j: (i, j)),
        compiler_params=_params(2),
        name=name,
    )(a, w, x)


def _out_proj_kernel(a_ref, w_ref, x_ref, g_ref, o_ref, h_ref):
    y = x_ref[...] + jnp.dot(a_ref[...], w_ref[...], preferred_element_type=F32)
    o_ref[...] = y
    yn = y * lax.rsqrt(jnp.mean(y * y, axis=-1, keepdims=True) + NORM_EPS)
    h_ref[...] = (yn * g_ref[...]).astype(h_ref.dtype)


def _out_proj(a, w, x, g_ffn3, l, tm):
    t, k = a.shape
    d = w.shape[2]
    tm = _tile(t, tm)
    return pl.pallas_call(
        _out_proj_kernel,
        out_shape=(jax.ShapeDtypeStruct((t, d), F32), jax.ShapeDtypeStruct((t, d), BF16)),
        grid=(t // tm,),
        in_specs=[
            pl.BlockSpec((tm, k), lambda i: (i, 0)),
            pl.BlockSpec((None, k, d), lambda i: (l, 0, 0)),
            pl.BlockSpec((tm, d), lambda i: (i, 0)),
            pl.BlockSpec((None, 1, d), lambda i: (l, 0, 0)),
        ],
        out_specs=(pl.BlockSpec((tm, d), lambda i: (i, 0)), pl.BlockSpec((tm, d), lambda i: (i, 0))),
        compiler_params=_params(1),
        name="out_proj",
    )(a, w, x, g_ffn3)


def _ffn_in_kernel(h_ref, wg_ref, wu_ref, o_ref, wgb_ref, wub_ref):
    @pl.when(pl.program_id(1) == 0)
    def _():
        wgb_ref[...] = wg_ref[...].astype(wgb_ref.dtype)
        wub_ref[...] = wu_ref[...].astype(wub_ref.dtype)

    h = h_ref[...]
    gate = jnp.dot(h, wgb_ref[...], preferred_element_type=F32)
    up = jnp.dot(h, wub_ref[...], preferred_element_type=F32)
    o_ref[...] = ((gate * _sigmoid(gate)) * up).astype(o_ref.dtype)


def _ffn_in(h, w_ffn_in, l, tm, tn):
    t, d = h.shape
    d_ff = w_ffn_in.shape[2] // 2
    tm = _tile(t, tm)
    tn = _tile(d_ff, tn)
    nj = d_ff // tn
    return pl.pallas_call(
        _ffn_in_kernel,
        out_shape=jax.ShapeDtypeStruct((t, d_ff), BF16),
        grid=(nj, t // tm),
        in_specs=[
            pl.BlockSpec((tm, d), lambda j, i: (i, 0)),
            pl.BlockSpec((None, d, tn), lambda j, i: (l, 0, j)),
            pl.BlockSpec((None, d, tn), lambda j, i: (l, 0, nj + j)),
        ],
        out_specs=pl.BlockSpec((tm, tn), lambda j, i: (i, j)),
        scratch_shapes=[pltpu.VMEM((d, tn), BF16), pltpu.VMEM((d, tn), BF16)],
        compiler_params=_params(2),
        name="ffn_in",
    )(h, w_ffn_in, w_ffn_in)


def _final_norm_kernel(x_ref, g_ref, o_ref):
    _rmsnorm_to(x_ref, g_ref, o_ref)


def _final_norm(x, g2, tm):
    t, d = x.shape
    tm = _tile(t, tm)
    return pl.pallas_call(
        _final_norm_kernel,
        out_shape=jax.ShapeDtypeStruct((t, d), F32),
        grid=(t // tm,),
        in_specs=[pl.BlockSpec((tm, d), lambda i: (i, 0)),
                  pl.BlockSpec((1, d), lambda i: (0, 0))],
        out_specs=pl.BlockSpec((tm, d), lambda i: (i, 0)),
        compiler_params=_params(1),
        name="final_norm",
    )(x, g2)


def _lambda_init(layer):
    return 0.8 - 0.6 * float(np.exp(-0.3 * layer))


def kernel(x_prompt, x_sample, g_mix, w_in, b_gate, lambda_q, lambda_k, g_subln, w_attn_proj, w_dw, b_dw,
           g_conv_ln, b_conv_ln, w_conv_proj, b_conv_proj, w_out, g_ffn, w_ffn_in, w_ffn_out, g_final):
    d = x_prompt.shape[2]
    depth = w_in.shape[0]
    head_dim = lambda_q.shape[2]
    v_width = w_attn_proj.shape[1]
    n_heads = v_width // (2 * head_dim)
    qk_width = n_heads * 2 * head_dim
    conv_ch = w_conv_proj.shape[1]
    gate_w = b_gate.shape[1]
    qkv_w = 2 * qk_width + v_width
    assert w_in.shape[2] == qkv_w + 2 * conv_ch + gate_w and gate_w == 2 * d

    slopes = jnp.asarray(2.0 ** (-8.0 * np.arange(1, n_heads + 1) / n_heads), dtype=F32)
    qkv_scale = jnp.asarray(np.concatenate([np.full(qk_width, head_dim ** -0.5 * LOG2E),
                                            np.ones(qkv_w - qk_width)])[None, :], dtype=F32)

    def row(p):
        return p.reshape(p.shape[0], 1, p.shape[1])

    w_qkv_b = _leading_columns_bf16(w_in, qkv_w, tr=TILES["cast_rows"])
    w_attn_b = w_attn_proj.astype(BF16)
    w_conv_b = w_conv_proj.astype(BF16)
    w_out_b = w_out.astype(BF16)
    w_ffn_out_b = w_ffn_out.astype(BF16)
    w_dw3 = w_dw.reshape(depth, w_dw.shape[1], conv_ch)
    g_mix3, b_gate3, b_dw3 = row(g_mix), row(b_gate), row(b_dw)
    g_subln_col = g_subln.reshape(depth, g_subln.shape[1], 1)
    g_ln3, b_ln3, b_pw3, g_ffn3 = row(g_conv_ln), row(b_conv_ln), row(b_conv_proj), row(g_ffn)
    g2 = g_final.reshape(1, d)

    def trunk(x3):
        n_seq, s_len, _ = x3.shape
        x = x3.reshape(n_seq * s_len, d)
        for l in range(depth):
            lam_init = _lambda_init(l)
            qkv, glu, gates = _in_proj(x, g_mix3, w_qkv_b, w_in, b_gate3, qkv_scale, l, (qkv_w, conv_ch, gate_w),
                                       *TILES["in_proj"])
            attn = _attention(qkv, slopes, lambda_q, lambda_k, g_subln_col, l, lam_init, n_seq, s_len, n_heads,
                              head_dim, rk=max(TW, s_len // TILES["attn_key_chunks"]))
            conv = _conv_module(glu, w_dw3, b_dw3, g_ln3, b_ln3, l, s_len, tt=TILES["conv_rows"])
            merged = _merge(attn, conv, w_attn_b, w_conv_b, b_pw3, gates, l, *TILES["merge"])
            x, h_ffn = _out_proj(merged, w_out_b, x, g_ffn3, l, tm=TILES["out_proj_rows"])
            act = _ffn_in(h_ffn, w_ffn_in, l, *TILES["ffn_in"])
            x = _residual_matmul(act, w_ffn_out_b, x, l, *TILES["ffn_out"], name="ffn_out")
        return _final_norm(x, g2, tm=TILES["norm_rows"]).reshape(x3.shape)

    return (trunk(x_prompt), trunk(x_sample))
```

```python
import functools

import numpy as np
import jax
import jax.numpy as jnp
from jax import lax
from jax.experimental import pallas as pl
from jax.experimental.pallas import tpu as pltpu

NORM_EPS = 1e-6
LN_EPS = 1e-5
LOG2E = 1.4426950408889634
SUBLANES = 8
LANES = 128
HALO = 16
VMEM_LIMIT = 52 * 1024 * 1024

F32 = jnp.float32
BF16 = jnp.bfloat16

TILES = {
    "in_proj": (1024, 1024),
    "merge": (1024, 512),
    "out_proj_rows": 512,
    "ffn_in": (1024, 512),
    "ffn_out": (1024, 512),
    "conv_rows": 256,
    "norm_rows": 512,
    "cast_rows": 256,
    "attn_key_chunks": 8,
}


def _params(n_axes):
    sem = ("parallel",) + ("arbitrary",) * (n_axes - 1)
    return pltpu.CompilerParams(dimension_semantics=sem, vmem_limit_bytes=VMEM_LIMIT)


def _tile(n, want):
    t = min(n, want)
    assert n % t == 0, (n, want)
    return t


def _cast_kernel(w_ref, o_ref):
    o_ref[...] = w_ref[...].astype(o_ref.dtype)


def _leading_columns_bf16(w, n_cols, tr):
    depth, k, _ = w.shape
    tr = _tile(k, tr)
    return pl.pallas_call(
        _cast_kernel,
        out_shape=jax.ShapeDtypeStruct((depth, k, n_cols), BF16),
        grid=(depth, k // tr),
        in_specs=[pl.BlockSpec((None, tr, n_cols), lambda l, r: (l, r, 0))],
        out_specs=pl.BlockSpec((None, tr, n_cols), lambda l, r: (l, r, 0)),
        compiler_params=_params(2),
        name="cast_qkv_weights",
    )(w)


def _sigmoid(x):
    return 0.5 * jnp.tanh(0.5 * x) + 0.5


def _rmsnorm_to(x_ref, g_ref, h_ref):
    xf = x_ref[...]
    y = xf * lax.rsqrt(jnp.mean(xf * xf, axis=-1, keepdims=True) + NORM_EPS)
    h_ref[...] = (y * g_ref[...]).astype(h_ref.dtype)


def _qkv_kernel(x_ref, g_ref, w_ref, cs_ref, o_ref, h_ref):
    @pl.when(pl.program_id(1) == 0)
    def _():
        _rmsnorm_to(x_ref, g_ref, h_ref)

    z = jnp.dot(h_ref[...], w_ref[...], preferred_element_type=F32)
    o_ref[...] = (z * cs_ref[...]).astype(o_ref.dtype)


def _glu_kernel(h_ref, wv_ref, wg_ref, o_ref, wvb_ref, wgb_ref):
    @pl.when(pl.program_id(1) == 0)
    def _():
        wvb_ref[...] = wv_ref[...].astype(wvb_ref.dtype)
        wgb_ref[...] = wg_ref[...].astype(wgb_ref.dtype)

    h = h_ref[...]
    val = jnp.dot(h, wvb_ref[...], preferred_element_type=F32)
    gate = jnp.dot(h, wgb_ref[...], preferred_element_type=F32)
    o_ref[...] = (val * _sigmoid(gate)).astype(o_ref.dtype)


def _gates_kernel(h_ref, w_ref, b_ref, o_ref, wb_ref):
    @pl.when(pl.program_id(1) == 0)
    def _():
        wb_ref[...] = w_ref[...].astype(wb_ref.dtype)

    z = jnp.dot(h_ref[...], wb_ref[...], preferred_element_type=F32)
    o_ref[...] = _sigmoid(z + b_ref[...]).astype(o_ref.dtype)


def _in_proj(x, g_mix3, w_qkv, w_in, b_gate3, qkv_scale, l, widths, tm, tn):
    t, d = x.shape
    qkv_w, conv_ch, gate_w = widths
    tm = _tile(t, tm)
    tq = _tile(qkv_w, tn)
    qkv, h = pl.pallas_call(
        _qkv_kernel,
        out_shape=(jax.ShapeDtypeStruct((t, qkv_w), BF16), jax.ShapeDtypeStruct((t, d), BF16)),
        grid=(t // tm, qkv_w // tq),
        in_specs=[pl.BlockSpec((tm, d), lambda i, j: (i, 0)),
                  pl.BlockSpec((None, 1, d), lambda i, j: (l, 0, 0)),
                  pl.BlockSpec((None, d, tq), lambda i, j: (l, 0, j)),
                  pl.BlockSpec((1, tq), lambda i, j: (0, j))],
        out_specs=(pl.BlockSpec((tm, tq), lambda i, j: (i, j)), pl.BlockSpec((tm, d), lambda i, j: (i, 0))),
        compiler_params=_params(2),
        name="in_proj_qkv",
    )(x, g_mix3, w_qkv, qkv_scale)

    h_spec = pl.BlockSpec((tm, d), lambda j, i: (i, 0))

    def w_spec(width, col0):
        assert col0 % width == 0
        return pl.BlockSpec((None, d, width), lambda j, i: (l, 0, col0 // width + j))

    tc = _tile(conv_ch, tn // 2)
    glu = pl.pallas_call(
        _glu_kernel,
        out_shape=jax.ShapeDtypeStruct((t, conv_ch), F32),
        grid=(conv_ch // tc, t // tm),
        in_specs=[h_spec, w_spec(tc, qkv_w), w_spec(tc, qkv_w + conv_ch)],
        out_specs=pl.BlockSpec((tm, tc), lambda j, i: (i, j)),
        scratch_shapes=[pltpu.VMEM((d, tc), BF16), pltpu.VMEM((d, tc), BF16)],
        compiler_params=_params(2),
        name="in_proj_glu",
    )(h, w_in, w_in)

    tg = _tile(gate_w, tn)
    gates = pl.pallas_call(
        _gates_kernel,
        out_shape=jax.ShapeDtypeStruct((t, gate_w), BF16),
        grid=(gate_w // tg, t // tm),
        in_specs=[h_spec, w_spec(tg, qkv_w + 2 * conv_ch),
                  pl.BlockSpec((None, 1, tg), lambda j, i: (l, 0, j))],
        out_specs=pl.BlockSpec((tm, tg), lambda j, i: (i, j)),
        scratch_shapes=[pltpu.VMEM((d, tg), BF16)],
        compiler_params=_params(2),
        name="in_proj_gates",
    )(h, w_in, b_gate3)
    return qkv, glu, gates


TW = 256
ATTN_SCORE_BYTES = 24 * 1024 * 1024


def _attn_kernel(slopes_ref, lq_ref, lk_ref, gcol_ref, q_ref, k_ref, v_ref, o_ref,
                 tab_ref, vt_ref, qbd_ref, st_ref, pt_ref, acc_ref, mb_ref, *, lam_init, head_dim, rk, n_sub):
    hd = head_dim
    vd = 2 * hd
    h = pl.program_id(0)
    b = pl.program_id(1)
    qi = pl.program_id(2)
    s_len = k_ref.shape[0]
    n_chunks = s_len // rk
    nt = 2 * TW

    @pl.when((b == 0) & (qi == 0))
    def _():
        neg = -(slopes_ref[h] * LOG2E)
        base = (lax.broadcasted_iota(jnp.int32, (rk, TW), 0)
                - lax.broadcasted_iota(jnp.int32, (rk, TW), 1) - (s_len - TW))
        n_rows = tab_ref.shape[0]
        for r0 in range(0, n_rows, rk):
            rw = min(rk, n_rows - r0)
            tab_ref[r0:r0 + rw, :] = neg * jnp.abs(base[:rw] + r0).astype(F32)

    @pl.when(qi == 0)
    def _():
        for r0 in range(0, s_len, rk):
            vt_ref[:, r0:r0 + rk] = v_ref[r0:r0 + rk, :].T

    lq = lq_ref[...]
    lk = lk_ref[...]
    lam = (jnp.exp(jnp.sum(lq[0:1] * lk[0:1], axis=-1, keepdims=True))
           - jnp.exp(jnp.sum(lq[1:2] * lk[1:2], axis=-1, keepdims=True)) + lam_init)

    zero = jnp.zeros((TW, hd), q_ref.dtype)
    for u in range(n_sub):
        qbd_ref[u, 0:TW, 0:hd] = q_ref[u * TW:(u + 1) * TW, 0:hd]
        qbd_ref[u, 0:TW, hd:vd] = zero
        qbd_ref[u, TW:nt, 0:hd] = zero
        qbd_ref[u, TW:nt, hd:vd] = q_ref[u * TW:(u + 1) * TW, hd:vd]

    def fold(x):
        return x.reshape(rk // SUBLANES, SUBLANES, x.shape[-1])

    def scores(u, j, mx):
        rows = slice(j * rk, (j + 1) * rk)
        off = s_len - TW - (qi * n_sub * TW + u * TW)
        sc = lax.dot_general(k_ref[rows, :], qbd_ref[u], (((1,), (1,)), ((), ())),
                             preferred_element_type=F32)
        tabv = tab_ref[pl.ds(pl.multiple_of(off + j * rk, TW), rk), :]
        mx = list(mx)
        for m in range(2):
            t = sc[:, m * TW:(m + 1) * TW] + tabv
            st_ref[u, rows, m * TW:(m + 1) * TW] = t
            for r in range(0, rk, SUBLANES):
                mx[m] = jnp.maximum(mx[m], t[r:r + SUBLANES])
        return tuple(mx)

    def finish_max(u, mx):
        for m in range(2):
            mb_ref[u, :, m * TW:(m + 1) * TW] = jnp.broadcast_to(
                jnp.max(mx[m], axis=0, keepdims=True), (SUBLANES, TW))

    def exps(u, j, ls):
        rows = slice(j * rk, (j + 1) * rk)
        p = jnp.exp2(fold(st_ref[u, rows, :]) - mb_ref[u][None])
        pt_ref[u, rows, :] = p.reshape(rk, nt).astype(pt_ref.dtype)
        return ls + jnp.sum(p, axis=0)

    def pv(u, j):
        keys = slice(j * rk, (j + 1) * rk)
        acc_ref[u] += jnp.dot(vt_ref[:, keys], pt_ref[u, keys, :], preferred_element_type=F32)

    def finish(u, ls):
        l = jnp.sum(ls, axis=0, keepdims=True)
        ot2 = acc_ref[u]
        ot = ot2[:, :TW] * (1.0 / l[:, :TW]) - ot2[:, TW:] * (lam / l[:, TW:])
        yt = ot * lax.rsqrt(jnp.mean(ot * ot, axis=0, keepdims=True) + NORM_EPS)
        yt = (yt * gcol_ref[...]) * (1.0 - lam_init)
        o_ref[u * TW:(u + 1) * TW, :] = yt.T.astype(o_ref.dtype)

    neg_inf = jnp.full((SUBLANES, TW), -jnp.inf, F32)
    acc_ref[...] = jnp.zeros_like(acc_ref)

    maxima = [(neg_inf, neg_inf)] * n_sub
    sums = [jnp.zeros((SUBLANES, nt), F32)] * n_sub
    for s in range(n_sub + 2):
        for j in range(n_chunks):
            if s < n_sub:
                maxima[s] = scores(s, j, maxima[s])
            if 0 <= s - 1 < n_sub:
                sums[s - 1] = exps(s - 1, j, sums[s - 1])
            if 0 <= s - 2 < n_sub:
                pv(s - 2, j)
        if s < n_sub:
            finish_max(s, maxima[s])
    for u in range(n_sub):
        finish(u, sums[u])


def _attention(qkv, slopes, lambda_q, lambda_k, g_subln_col, l, lam_init, n_seq, s_len, n_heads, head_dim, rk):
    v_dim = 2 * head_dim
    n_sub = max(2, min(4, s_len // TW, ATTN_SCORE_BYTES // (s_len * 2 * TW * 6)))
    tq = n_sub * TW
    assert s_len % tq == 0 and v_dim == TW
    rk = _tile(s_len, rk)
    nq = s_len // tq
    kern = functools.partial(_attn_kernel, lam_init=lam_init, head_dim=head_dim, rk=rk, n_sub=n_sub)
    return pl.pallas_call(
        kern,
        out_shape=jax.ShapeDtypeStruct((n_seq * s_len, n_heads * v_dim), BF16),
        grid=(n_heads, n_seq, nq),
        in_specs=[
            pl.BlockSpec(memory_space=pltpu.SMEM),
            pl.BlockSpec((None, 2, head_dim), lambda h, b, i: (l, 0, 0)),
            pl.BlockSpec((None, 2, head_dim), lambda h, b, i: (l, 0, 0)),
            pl.BlockSpec((None, v_dim, 1), lambda h, b, i: (l, 0, 0)),
            pl.BlockSpec((tq, v_dim), lambda h, b, i: (b * nq + i, h)),
            pl.BlockSpec((s_len, v_dim), lambda h, b, i: (b, n_heads + h)),
            pl.BlockSpec((s_len, v_dim), lambda h, b, i: (b, 2 * n_heads + h)),
        ],
        out_specs=pl.BlockSpec((tq, v_dim), lambda h, b, i: (b * nq + i, h)),
        scratch_shapes=[
            pltpu.VMEM((2 * s_len - TW, TW), F32),
            pltpu.VMEM((v_dim, s_len), BF16),
            pltpu.VMEM((n_sub, 2 * TW, v_dim), BF16),
            pltpu.VMEM((n_sub, s_len, 2 * TW), F32),
            pltpu.VMEM((n_sub, s_len, 2 * TW), BF16),
            pltpu.VMEM((n_sub, v_dim, 2 * TW), F32),
            pltpu.VMEM((n_sub, SUBLANES, 2 * TW), F32),
        ],
        compiler_params=pltpu.CompilerParams(dimension_semantics=("arbitrary",) * 3,
                                             vmem_limit_bytes=VMEM_LIMIT),
        name="diff_attention",
    )(slopes, lambda_q, lambda_k, g_subln_col, qkv, qkv, qkv)


def _conv_kernel(prev_ref, main_ref, next_ref, w_ref, bdw_ref, gln_ref, bln_ref, o_ref, win_ref, sh_ref, acc_ref,
                 *, tt, n_taps, tiles_per_seq, rows, lanes):
    pos = pl.program_id(0) % tiles_per_seq
    win_ref[0:HALO, :] = jnp.where(pos == 0, 0.0, prev_ref[...])
    win_ref[HALO:HALO + tt, :] = main_ref[...]
    win_ref[HALO + tt:, :] = jnp.where(pos == tiles_per_seq - 1, 0.0, next_ref[...])

    span = sh_ref.shape[1]
    for s in range(1, SUBLANES):
        sh_ref[s - 1] = win_ref[s:s + span, :]

    first = HALO - (n_taps - 1) // 2
    def lane_chunk(ci, carry):
        cols = pl.ds(pl.multiple_of(ci * lanes, lanes), lanes)
        for r0 in range(0, tt, rows):
            acc = None
            for s in range(SUBLANES):
                offs = [d for d in range(first, first + n_taps) if d % SUBLANES == s]
                if not offs:
                    continue
                lo, hi = offs[0] - s, offs[-1] - s
                span_rows = slice(r0 + lo, r0 + hi + rows)
                slab = win_ref[span_rows, cols] if s == 0 else sh_ref[s - 1, span_rows, cols]
                for d in offs:
                    k = d - first
                    term = w_ref[k:k + 1, cols] * slab[d - s - lo:d - s - lo + rows]
                    acc = term if acc is None else acc + term
            acc_ref[r0:r0 + rows, cols] = acc + bdw_ref[:, cols]
        return carry

    lax.fori_loop(0, main_ref.shape[1] // lanes, lane_chunk, 0)

    y = acc_ref[...]
    yc = y - jnp.mean(y, axis=-1, keepdims=True)
    yn = yc * lax.rsqrt(jnp.mean(yc * yc, axis=-1, keepdims=True) + LN_EPS)
    z = yn * gln_ref[...] + bln_ref[...]
    o_ref[...] = (z * _sigmoid(z)).astype(o_ref.dtype)


def _conv_module(glu, w_dw, b_dw3, g_ln3, b_ln3, l, s_len, tt):
    t, c = glu.shape
    n_taps = w_dw.shape[1]
    pad = (n_taps - 1) // 2
    assert pad <= HALO
    tt = _tile(s_len, tt)
    assert tt % HALO == 0 and t % s_len == 0
    hb = tt // HALO
    n_hblk = t // HALO
    span = tt + (HALO + pad) // SUBLANES * SUBLANES
    vec = pl.BlockSpec((None, 1, c), lambda i: (l, 0, 0))
    kern = functools.partial(_conv_kernel, tt=tt, n_taps=n_taps, tiles_per_seq=s_len // tt, rows=64, lanes=LANES)
    return pl.pallas_call(
        kern,
        out_shape=jax.ShapeDtypeStruct((t, c), BF16),
        grid=(t // tt,),
        in_specs=[
            pl.BlockSpec((HALO, c), lambda i: (jnp.maximum(i * hb - 1, 0), 0)),
            pl.BlockSpec((tt, c), lambda i: (i, 0)),
            pl.BlockSpec((HALO, c), lambda i: (jnp.minimum((i + 1) * hb, n_hblk - 1), 0)),
            pl.BlockSpec((None, n_taps, c), lambda i: (l, 0, 0)),
            vec, vec, vec,
        ],
        out_specs=pl.BlockSpec((tt, c), lambda i: (i, 0)),
        scratch_shapes=[pltpu.VMEM((tt + 2 * HALO, c), F32),
                        pltpu.VMEM((SUBLANES - 1, span, c), F32),
                        pltpu.VMEM((tt, c), F32)],
        compiler_params=_params(1),
        name="conv_module",
    )(glu, glu, glu, w_dw, b_dw3, g_ln3, b_ln3)


def _merge_kernel(a_ref, c_ref, wa_ref, wc_ref, bc_ref, ga_ref, gc_ref, o_ref, wab_ref, wcb_ref):
    @pl.when(pl.program_id(1) == 0)
    def _():
        wab_ref[...] = wa_ref[...].astype(wab_ref.dtype)
        wcb_ref[...] = wc_ref[...].astype(wcb_ref.dtype)

    a = jnp.dot(a_ref[...], wab_ref[...], preferred_element_type=F32)
    c = jnp.dot(c_ref[...], wcb_ref[...], preferred_element_type=F32) + bc_ref[...]
    o_ref[...] = (ga_ref[...] * a + gc_ref[...] * c).astype(o_ref.dtype)


def _merge(attn, conv, w_attn_proj, w_conv_proj, b_conv_proj3, gates, l, tm, tn):
    t, kv = attn.shape
    kc = conv.shape[1]
    d = w_attn_proj.shape[2]
    tm = _tile(t, tm)
    tn = _tile(d, tn)
    nj = d // tn
    return pl.pallas_call(
        _merge_kernel,
        out_shape=jax.ShapeDtypeStruct((t, d), BF16),
        grid=(nj, t // tm),
        in_specs=[
            pl.BlockSpec((tm, kv), lambda j, i: (i, 0)),
            pl.BlockSpec((tm, kc), lambda j, i: (i, 0)),
            pl.BlockSpec((None, kv, tn), lambda j, i: (l, 0, j)),
            pl.BlockSpec((None, kc, tn), lambda j, i: (l, 0, j)),
            pl.BlockSpec((None, 1, tn), lambda j, i: (l, 0, j)),
            pl.BlockSpec((tm, tn), lambda j, i: (i, j)),
            pl.BlockSpec((tm, tn), lambda j, i: (i, nj + j)),
        ],
        out_specs=pl.BlockSpec((tm, tn), lambda j, i: (i, j)),
        scratch_shapes=[pltpu.VMEM((kv, tn), BF16), pltpu.VMEM((kc, tn), BF16)],
        compiler_params=_params(2),
        name="branch_merge",
    )(attn, conv, w_attn_proj, w_conv_proj, b_conv_proj3, gates, gates)


def _residual_matmul_kernel(a_ref, w_ref, x_ref, o_ref):
    o_ref[...] = x_ref[...] + jnp.dot(a_ref[...], w_ref[...], preferred_element_type=F32)


def _residual_matmul(a, w, x, l, tm, tn, name):
    t, k = a.shape
    d = w.shape[2]
    tm = _tile(t, tm)
    tn = _tile(d, tn)
    return pl.pallas_call(
        _residual_matmul_kernel,
        out_shape=jax.ShapeDtypeStruct((t, d), F32),
        grid=(t // tm, d // tn),
        in_specs=[
            pl.BlockSpec((tm, k), lambda i, j: (i, 0)),
            pl.BlockSpec((None, k, tn), lambda i, j: (l, 0, j)),
            pl.BlockSpec((tm, tn), lambda i, j: (i, j)),
        ],
        out_specs=pl.BlockSpec((tm, tn), lambda i, j: (i, j)),
        compiler_params=_params(2),
        name=name,
    )(a, w, x)


def _out_proj_kernel(a_ref, w_ref, x_ref, g_ref, o_ref, h_ref, wb_ref):
    @pl.when(pl.program_id(0) == 0)
    def _():
        wb_ref[...] = w_ref[...].astype(wb_ref.dtype)

    y = x_ref[...] + jnp.dot(a_ref[...], wb_ref[...], preferred_element_type=F32)
    o_ref[...] = y
    yn = y * lax.rsqrt(jnp.mean(y * y, axis=-1, keepdims=True) + NORM_EPS)
    h_ref[...] = (yn * g_ref[...]).astype(h_ref.dtype)


def _out_proj(a, w, x, g_ffn3, l, tm):
    t, k = a.shape
    d = w.shape[2]
    tm = _tile(t, tm)
    return pl.pallas_call(
        _out_proj_kernel,
        out_shape=(jax.ShapeDtypeStruct((t, d), F32), jax.ShapeDtypeStruct((t, d), BF16)),
        grid=(t // tm,),
        in_specs=[
            pl.BlockSpec((tm, k), lambda i: (i, 0)),
            pl.BlockSpec((None, k, d), lambda i: (l, 0, 0), pipeline_mode=pl.Buffered(1)),
            pl.BlockSpec((tm, d), lambda i: (i, 0)),
            pl.BlockSpec((None, 1, d), lambda i: (l, 0, 0)),
        ],
        out_specs=(pl.BlockSpec((tm, d), lambda i: (i, 0)), pl.BlockSpec((tm, d), lambda i: (i, 0))),
        scratch_shapes=[pltpu.VMEM((k, d), BF16)],
        compiler_params=pltpu.CompilerParams(dimension_semantics=("arbitrary",), vmem_limit_bytes=VMEM_LIMIT),
        name="out_proj",
    )(a, w, x, g_ffn3)


def _ffn_in_kernel(h_ref, wg_ref, wu_ref, o_ref, wgb_ref, wub_ref):
    @pl.when(pl.program_id(1) == 0)
    def _():
        wgb_ref[...] = wg_ref[...].astype(wgb_ref.dtype)
        wub_ref[...] = wu_ref[...].astype(wub_ref.dtype)

    h = h_ref[...]
    gate = jnp.dot(h, wgb_ref[...], preferred_element_type=F32)
    up = jnp.dot(h, wub_ref[...], preferred_element_type=F32)
    o_ref[...] = ((gate * _sigmoid(gate)) * up).astype(o_ref.dtype)


def _ffn_in(h, w_ffn_in, l, tm, tn):
    t, d = h.shape
    d_ff = w_ffn_in.shape[2] // 2
    tm = _tile(t, tm)
    tn = _tile(d_ff, tn)
    nj = d_ff // tn
    return pl.pallas_call(
        _ffn_in_kernel,
        out_shape=jax.ShapeDtypeStruct((t, d_ff), BF16),
        grid=(nj, t // tm),
        in_specs=[
            pl.BlockSpec((tm, d), lambda j, i: (i, 0)),
            pl.BlockSpec((None, d, tn), lambda j, i: (l, 0, j)),
            pl.BlockSpec((None, d, tn), lambda j, i: (l, 0, nj + j)),
        ],
        out_specs=pl.BlockSpec((tm, tn), lambda j, i: (i, j)),
        scratch_shapes=[pltpu.VMEM((d, tn), BF16), pltpu.VMEM((d, tn), BF16)],
        compiler_params=_params(2),
        name="ffn_in",
    )(h, w_ffn_in, w_ffn_in)


def _final_norm_kernel(x_ref, g_ref, o_ref):
    _rmsnorm_to(x_ref, g_ref, o_ref)


def _final_norm(x, g2, tm):
    t, d = x.shape
    tm = _tile(t, tm)
    return pl.pallas_call(
        _final_norm_kernel,
        out_shape=jax.ShapeDtypeStruct((t, d), F32),
        grid=(t // tm,),
        in_specs=[pl.BlockSpec((tm, d), lambda i: (i, 0)),
                  pl.BlockSpec((1, d), lambda i: (0, 0))],
        out_specs=pl.BlockSpec((tm, d), lambda i: (i, 0)),
        compiler_params=_params(1),
        name="final_norm",
    )(x, g2)


def _lambda_init(layer):
    return 0.8 - 0.6 * float(np.exp(-0.3 * layer))


def kernel(x_prompt, x_sample, g_mix, w_in, b_gate, lambda_q, lambda_k, g_subln, w_attn_proj, w_dw, b_dw,
           g_conv_ln, b_conv_ln, w_conv_proj, b_conv_proj, w_out, g_ffn, w_ffn_in, w_ffn_out, g_final):
    d = x_prompt.shape[2]
    depth = w_in.shape[0]
    head_dim = lambda_q.shape[2]
    v_width = w_attn_proj.shape[1]
    n_heads = v_width // (2 * head_dim)
    qk_width = n_heads * 2 * head_dim
    conv_ch = w_conv_proj.shape[1]
    gate_w = b_gate.shape[1]
    qkv_w = 2 * qk_width + v_width
    assert w_in.shape[2] == qkv_w + 2 * conv_ch + gate_w and gate_w == 2 * d

    slopes = jnp.asarray(2.0 ** (-8.0 * np.arange(1, n_heads + 1) / n_heads), dtype=F32)
    qkv_scale = jnp.asarray(np.concatenate([np.full(qk_width, head_dim ** -0.5 * LOG2E),
                                            np.ones(qkv_w - qk_width)])[None, :], dtype=F32)

    def row(p):
        return p.reshape(p.shape[0], 1, p.shape[1])

    w_qkv_b = _leading_columns_bf16(w_in, qkv_w, tr=TILES["cast_rows"])
    w_ffn_out_b = w_ffn_out.astype(BF16)
    w_dw3 = w_dw.reshape(depth, w_dw.shape[1], conv_ch)
    g_mix3, b_gate3, b_dw3 = row(g_mix), row(b_gate), row(b_dw)
    g_subln_col = g_subln.reshape(depth, g_subln.shape[1], 1)
    g_ln3, b_ln3, b_pw3, g_ffn3 = row(g_conv_ln), row(b_conv_ln), row(b_conv_proj), row(g_ffn)
    g2 = g_final.reshape(1, d)

    def trunk(x3):
        n_seq, s_len, _ = x3.shape
        x = x3.reshape(n_seq * s_len, d)
        for l in range(depth):
            lam_init = _lambda_init(l)
            qkv, glu, gates = _in_proj(x, g_mix3, w_qkv_b, w_in, b_gate3, qkv_scale, l, (qkv_w, conv_ch, gate_w),
                                       *TILES["in_proj"])
            attn = _attention(qkv, slopes, lambda_q, lambda_k, g_subln_col, l, lam_init, n_seq, s_len, n_heads,
                              head_dim, rk=max(TW, s_len // TILES["attn_key_chunks"]))
            conv = _conv_module(glu, w_dw3, b_dw3, g_ln3, b_ln3, l, s_len, tt=TILES["conv_rows"])
            merged = _merge(attn, conv, w_attn_proj, w_conv_proj, b_pw3, gates, l, *TILES["merge"])
            x, h_ffn = _out_proj(merged, w_out, x, g_ffn3, l, tm=TILES["out_proj_rows"])
            act = _ffn_in(h_ffn, w_ffn_in, l, *TILES["ffn_in"])
            x = _residual_matmul(act, w_ffn_out_b, x, l, *TILES["ffn_out"], name="ffn_out")
        return _final_norm(x, g2, tm=TILES["norm_rows"]).reshape(x3.shape)

    return (trunk(x_prompt), trunk(x_sample))
```
